```python
import math
import jax, jax.numpy as jnp
from jax import lax
import numpy as np

D_MODEL = 1024
BATCH = 8
SEQ = 4096
DEPTH = 2

DIFF_HEADS = 4
DIFF_HD = 64
DIFF_W = DIFF_HEADS * 2 * DIFF_HD
MLA_HEADS = 8
MLA_NOPE = 64
MLA_ROPE = 32
MLA_V = 64
MLA_QK = MLA_NOPE + MLA_ROPE
Q_LORA = 256
KV_LORA = 128
MLA_W = MLA_HEADS * MLA_V
ROPE_THETA = 10000.0
CONV_W = 512
CONV_K = 3
N_BRANCH = 3
PEER_HEADS = 8
N_KEYS = 128
N_EXPERTS = N_KEYS * N_KEYS
PEER_DQ = 256
PEER_DK = PEER_DQ // 2
K_SUB = 16
PEER_TOPK = 16
PEER_CHUNK = 128
PLE_DIM = 256
BLOCK_Q = 128
NORM_EPS = 1e-6
NEG_INF = -1e30
IN_SPLITS = (DIFF_W, DIFF_W, DIFF_W, Q_LORA, KV_LORA, MLA_ROPE, CONV_W, CONV_W, CONV_W, N_BRANCH * D_MODEL)
D_IN = 3 * DIFF_W + Q_LORA + KV_LORA + MLA_ROPE + 3 * CONV_W + N_BRANCH * D_MODEL

kernel_name = "hybrid_diffattn_mla_shortconv_peer"


def _rms(x, g):
    xf = x.astype(jnp.float32)
    y = xf * lax.rsqrt(jnp.mean(xf * xf, axis=-1, keepdims=True) + NORM_EPS)
    return (y * g.astype(jnp.float32)).astype(x.dtype)


def _alibi_slopes(n):
    return 2.0 ** (-8.0 * jnp.arange(1, n + 1, dtype=jnp.float32) / n)


def _rope(x, pos):
    half = MLA_ROPE // 2
    freqs = ROPE_THETA ** (-jnp.arange(half, dtype=jnp.float32) / half)
    ang = pos.astype(jnp.float32)[:, None] * freqs[None, :]
    cos, sin = jnp.cos(ang), jnp.sin(ang)
    x1 = x[..., :half].astype(jnp.float32)
    x2 = x[..., half:].astype(jnp.float32)
    out = jnp.concatenate([x1 * cos - x2 * sin, x2 * cos + x1 * sin], axis=-1)
    return out.astype(x.dtype)


def _causal_probs(qb, k, start, scale, slopes):
    nq, nk = qb.shape[-2], k.shape[-2]
    s = jnp.einsum('bhmqd,bhmkd->bhmqk', qb, k).astype(jnp.float32) * scale
    dist = (start + jnp.arange(nq))[:, None] - jnp.arange(nk)[None, :]
    if slopes is not None:
        s = s - slopes[None, :, None, None, None] * jnp.abs(dist).astype(jnp.float32)
    s = jnp.where(dist >= 0, s, NEG_INF)
    return jax.nn.softmax(s, axis=-1)


def _sweep_blocks(block_fn, seq):
    out = lax.map(block_fn, jnp.arange(seq // BLOCK_Q))
    out = jnp.moveaxis(out, 0, 2)
    return out.reshape(out.shape[0], out.shape[1], seq, out.shape[-1])


def _diff_attention(q_raw, k_raw, v_raw, g_q, g_k, lq1, lk1, lq2, lk2, g_sub, lambda_init):
    B, S, _ = q_raw.shape
    q = _rms(q_raw.reshape(B, S, DIFF_HEADS, 2, DIFF_HD), g_q).transpose(0, 2, 3, 1, 4)
    k = _rms(k_raw.reshape(B, S, DIFF_HEADS, 2, DIFF_HD), g_k).transpose(0, 2, 3, 1, 4)
    v = v_raw.reshape(B, S, DIFF_HEADS, 2 * DIFF_HD).transpose(0, 2, 1, 3)
    lam = (jnp.exp(jnp.sum(lq1.astype(jnp.float32) * lk1.astype(jnp.float32)))
           - jnp.exp(jnp.sum(lq2.astype(jnp.float32) * lk2.astype(jnp.float32)))
           + lambda_init)
    slopes = _alibi_slopes(DIFF_HEADS)
    scale = DIFF_HD ** -0.5

    def block(i):
        start = i * BLOCK_Q
        qb = lax.dynamic_slice_in_dim(q, start, BLOCK_Q, axis=3)
        p = _causal_probs(qb, k, start, scale, slopes)
        a = p[:, :, 0] - lam * p[:, :, 1]
        return jnp.einsum('bhqk,bhkd->bhqd', a.astype(v.dtype), v)

    o = _sweep_blocks(block, S)
    o = _rms(o, g_sub) * (1.0 - lambda_init)
    return o.transpose(0, 2, 1, 3).reshape(B, S, DIFF_W)


def _mla(c_q, c_kv, k_rope, g_cq, w_uq, g_ckv, w_ukv, g_q, g_k):
    B, S, _ = c_q.shape
    pos = jnp.arange(S)
    q = (_rms(c_q, g_cq) @ w_uq).reshape(B, S, MLA_HEADS, MLA_QK)
    kv = (_rms(c_kv, g_ckv) @ w_ukv).reshape(B, S, MLA_HEADS, MLA_NOPE + MLA_V)
    k_nope, v = kv[..., :MLA_NOPE], kv[..., MLA_NOPE:]
    k = jnp.concatenate([k_nope, jnp.broadcast_to(k_rope[:, :, None, :], (B, S, MLA_HEADS, MLA_ROPE))], axis=-1)
    q = _rms(q, g_q).transpose(0, 2, 1, 3)
    k = _rms(k, g_k).transpose(0, 2, 1, 3)
    q = jnp.concatenate([q[..., :MLA_NOPE], _rope(q[..., MLA_NOPE:], pos)], axis=-1)[:, :, None]
    k = jnp.concatenate([k[..., :MLA_NOPE], _rope(k[..., MLA_NOPE:], pos)], axis=-1)[:, :, None]
    v = v.transpose(0, 2, 1, 3)
    scale = MLA_QK ** -0.5

    def block(i):
        start = i * BLOCK_Q
        qb = lax.dynamic_slice_in_dim(q, start, BLOCK_Q, axis=3)
        p = _causal_probs(qb, k, start, scale, None)[:, :, 0]
        return jnp.einsum('bhqk,bhkd->bhqd', p.astype(v.dtype), v)

    o = _sweep_blocks(block, S)
    return o.transpose(0, 2, 1, 3).reshape(B, S, MLA_W)


def _short_conv(b_gate, c_gate, xv, conv_w):
    u = c_gate * xv
    S = u.shape[1]
    up = jnp.pad(u, ((0, 0), (CONV_K - 1, 0), (0, 0)))
    y = conv_w[0] * up[:, 0:S]
    for j in range(1, CONV_K):
        y = y + conv_w[j] * up[:, j:j + S]
    return b_gate * y


def _peer(h, w_query, sub_keys, u_tab, v_tab):
    B, S, D = h.shape
    ht = h.reshape(-1, PEER_CHUNK, D)

    def chunk(hc):
        T = hc.shape[0]
        q = (hc @ w_query).reshape(T, PEER_HEADS, 2, PEER_DK)
        s = jnp.einsum('thpd,phnd->thpn', q, sub_keys).astype(jnp.float32)
        top_s, top_i = lax.top_k(s, K_SUB)
        cand = (top_s[:, :, 0, :, None] + top_s[:, :, 1, None, :]).reshape(T, PEER_HEADS, K_SUB * K_SUB)
        best_s, best_c = lax.top_k(cand, PEER_TOPK)
        i1 = jnp.take_along_axis(top_i[:, :, 0], best_c // K_SUB, axis=-1)
        i2 = jnp.take_along_axis(top_i[:, :, 1], best_c % K_SUB, axis=-1)
        expert = i1 * N_KEYS + i2
        gate = jax.nn.softmax(best_s, axis=-1)
        u_sel = u_tab[expert]
        act = jax.nn.gelu(jnp.einsum('thkd,td->thk', u_sel, hc), approximate=False)
        w = (gate * act.astype(jnp.float32)).astype(hc.dtype)
        return jnp.einsum('thk,thkd->td', w, v_tab[expert])

    return lax.map(chunk, ht).reshape(B, S, D)


def _layer(x, p_i, layer_idx, g_mix, w_in, g_diff_q, g_diff_k, lam_q1, lam_k1, lam_q2, lam_k2,
           g_diff_sub, g_cq, w_uq, g_ckv, w_ukv, g_mla_q, g_mla_k, conv_w, w_br_diff, w_br_mla,
           w_br_conv, w_out, g_ffn, w_query, sub_keys, u_experts, v_experts, g_ple, w_ple_gate,
           w_ple_proj):
    B, S, D = x.shape
    lambda_init = 0.8 - 0.6 * math.exp(-0.3 * layer_idx)
    h = _rms(x, g_mix)
    z = h @ w_in
    offsets = np.cumsum(IN_SPLITS)[:-1].tolist()
    dq, dk, dv, cq, ckv, kr, cb, cc, cx, gates = jnp.split(z, offsets, axis=-1)
    y_d = _diff_attention(dq, dk, dv, g_diff_q, g_diff_k, lam_q1, lam_k1, lam_q2, lam_k2, g_diff_sub, lambda_init)
    y_m = _mla(cq, ckv, kr, g_cq, w_uq, g_ckv, w_ukv, g_mla_q, g_mla_k)
    y_c = _short_conv(cb, cc, cx, conv_w)
    g = jax.nn.sigmoid(gates).reshape(B, S, N_BRANCH, D)
    mixed = g[:, :, 0] * (y_d @ w_br_diff) + g[:, :, 1] * (y_m @ w_br_mla) + g[:, :, 2] * (y_c @ w_br_conv)
    x = x + mixed @ w_out
    x = x + _peer(_rms(x, g_ffn), w_query, sub_keys, u_experts, v_experts)
    x = x + jax.nn.sigmoid(_rms(x, g_ple) @ w_ple_gate) * (p_i @ w_ple_proj)
    return x


def setup_inputs(seed: int = 0) -> dict:
    key = jax.random.key(seed)
    ks = jax.random.split(key, 30)
    L, D = DEPTH, D_MODEL

    def nrm(k, shape, scale):
        return jax.random.normal(k, shape, jnp.float32) * scale

    def gain(k, shape):
        return 1.0 + 0.05 * jax.random.normal(k, shape, jnp.float32)

    return {
        "x": nrm(ks[0], (BATCH, SEQ, D), 1.0),
        "p": nrm(ks[1], (DEPTH, BATCH, SEQ, PLE_DIM), 1.0),
        "g_mix": gain(ks[2], (L, D)),
        "w_in": nrm(ks[3], (L, D, D_IN), D ** -0.5),
        "g_diff_q": gain(ks[4], (L, DIFF_HD)),
        "g_diff_k": gain(ks[5], (L, DIFF_HD)),
        "lam_q1": nrm(ks[6], (L, DIFF_HD), 0.1),
        "lam_k1": nrm(ks[7], (L, DIFF_HD), 0.1),
        "lam_q2": nrm(ks[8], (L, DIFF_HD), 0.1),
        "lam_k2": nrm(ks[9], (L, DIFF_HD), 0.1),
        "g_diff_sub": gain(ks[10], (L, 2 * DIFF_HD)),
        "g_cq": gain(ks[11], (L, Q_LORA)),
        "w_uq": nrm(ks[12], (L, Q_LORA, MLA_HEADS * MLA_QK), Q_LORA ** -0.5),
        "g_ckv": gain(ks[13], (L, KV_LORA)),
        "w_ukv": nrm(ks[14], (L, KV_LORA, MLA_HEADS * (MLA_NOPE + MLA_V)), KV_LORA ** -0.5),
        "g_mla_q": gain(ks[15], (L, MLA_QK)),
        "g_mla_k": gain(ks[16], (L, MLA_QK)),
        "conv_w": nrm(ks[17], (L, CONV_K, CONV_W), CONV_K ** -0.5),
        "w_br_diff": nrm(ks[18], (L, DIFF_W, D), DIFF_W ** -0.5),
        "w_br_mla": nrm(ks[19], (L, MLA_W, D), MLA_W ** -0.5),
        "w_br_conv": nrm(ks[20], (L, CONV_W, D), CONV_W ** -0.5),
        "w_out": nrm(ks[21], (L, D, D), D ** -0.5),
        "g_ffn": gain(ks[22], (L, D)),
        "w_query": nrm(ks[23], (L, D, PEER_HEADS * PEER_DQ), D ** -0.5),
        "sub_keys": nrm(ks[24], (L, 2, PEER_HEADS, N_KEYS, PEER_DK), PEER_DK ** -0.5),
        "u_experts": nrm(ks[25], (L, N_EXPERTS, D), D ** -0.5),
        "v_experts": nrm(ks[26], (L, N_EXPERTS, D), PEER_HEADS ** -0.5),
        "g_ple": gain(ks[27], (L, D)),
        "w_ple_gate": nrm(ks[28], (L, D, D), D ** -0.5),
        "w_ple_proj": nrm(ks[29], (L, PLE_DIM, D), PLE_DIM ** -0.5),
    }


def reference(x, p, g_mix, w_in, g_diff_q, g_diff_k, lam_q1, lam_k1, lam_q2, lam_k2, g_diff_sub,
              g_cq, w_uq, g_ckv, w_ukv, g_mla_q, g_mla_k, conv_w, w_br_diff, w_br_mla, w_br_conv,
              w_out, g_ffn, w_query, sub_keys, u_experts, v_experts, g_ple, w_ple_gate, w_ple_proj):
    for i in range(DEPTH):
        x = _layer(x, p[i], i, g_mix[i], w_in[i], g_diff_q[i], g_diff_k[i], lam_q1[i], lam_k1[i],
                   lam_q2[i], lam_k2[i], g_diff_sub[i], g_cq[i], w_uq[i], g_ckv[i], w_ukv[i],
                   g_mla_q[i], g_mla_k[i], conv_w[i], w_br_diff[i], w_br_mla[i], w_br_conv[i],
                   w_out[i], g_ffn[i], w_query[i], sub_keys[i], u_experts[i], v_experts[i],
                   g_ple[i], w_ple_gate[i], w_ple_proj[i])
    return x
```

```python
import functools
import math

import jax
import jax.numpy as jnp
import numpy as np
from jax import lax
from jax.experimental import pallas as pl
from jax.experimental.pallas import tpu as pltpu

F32 = jnp.float32
BF16 = jnp.bfloat16
I32 = jnp.int32

D_MODEL = 1024
DIFF_HEADS = 4
DIFF_HD = 64
DIFF_W = 512
MLA_HEADS = 8
MLA_NOPE = 64
MLA_ROPE = 32
MLA_V = 64
MLA_QK = 96
Q_LORA = 256
KV_LORA = 128
ROPE_THETA = 10000.0
CONV_W = 512
CONV_K = 3
PEER_HEADS = 8
N_KEYS = 128
N_EXPERTS = N_KEYS * N_KEYS
PEER_DK = 128
K_SUB = 16
PEER_TOPK = 16
PEER_PAIRS = PEER_HEADS * PEER_TOPK
PLE_DIM = 256
NORM_EPS = 1e-6
NEG_INF = -1e30

LANES = 128
SUBLANES = 8
VMEM_LIMIT = 56 * 1024 * 1024

Z_GATES = 0
Z_DQ = 3072
Z_DK = 3584
Z_DV = 4096
Z_CB = 4608
Z_CC = 5120
Z_CX = 5632
Z_MLA = 6144
Z_W = 6656


def _cparams(sem, vmem=VMEM_LIMIT):
    return pltpu.CompilerParams(dimension_semantics=sem, vmem_limit_bytes=vmem)


def _rms(x, g):
    return x * lax.rsqrt(jnp.mean(x * x, axis=-1, keepdims=True) + NORM_EPS) * g


def _inproj_kernel(x_ref, g_ref, w_ref, z_ref):
    h = _rms(x_ref[...], g_ref[...]).astype(BF16)
    z_ref[...] = jnp.dot(h, w_ref[...], preferred_element_type=F32)


def _inproj(x2, g, w_in_r, tm):
    n = x2.shape[0]
    return pl.pallas_call(
        _inproj_kernel,
        grid=(n // tm,),
        in_specs=[
            pl.BlockSpec((tm, D_MODEL), lambda i: (i, 0)),
            pl.BlockSpec((1, D_MODEL), lambda i: (0, 0)),
            pl.BlockSpec((D_MODEL, Z_W), lambda i: (0, 0)),
        ],
        out_specs=pl.BlockSpec((tm, Z_W), lambda i: (i, 0)),
        out_shape=jax.ShapeDtypeStruct((n, Z_W), F32),
        compiler_params=_cparams(("parallel",)),
        name="inproj",
    )(x2, g, w_in_r)


def _softmax_step(s, m, l, a, vt):
    mn = jnp.maximum(m, jnp.max(s, axis=-1, keepdims=True))
    p = jnp.exp(s - mn)
    alpha = jnp.exp(m - mn)
    l = alpha * l + jnp.sum(p, axis=-1, keepdims=True)
    a = alpha * a + jnp.dot(p.astype(BF16), vt, preferred_element_type=F32)
    return mn, l, a


def _qk(q, kt):
    return lax.dot_general(q, kt, (((1,), (1,)), ((), ())), preferred_element_type=F32)


def _diff_attn_kernel(slope_ref, q_ref, k_ref, v_ref, gq_ref, gk_ref, lam_ref, gsub_ref,
                      o_ref, kn_ref, *, tq, seq, lambda_init):
    h = pl.program_id(1)
    qi = pl.program_id(2)
    lane = lax.broadcasted_iota(I32, (1, LANES), 1)
    first = lane < DIFF_HD

    def halfnorm(x, g):
        x2 = x * x
        s_all = jnp.sum(x2, axis=-1, keepdims=True)
        s_lo = jnp.sum(jnp.where(first, x2, 0.0), axis=-1, keepdims=True)
        ms = jnp.where(first, s_lo, s_all - s_lo) * (1.0 / DIFF_HD)
        return x * lax.rsqrt(ms + NORM_EPS) * g

    @pl.when(qi == 0)
    def _():
        def body(c, carry):
            r0 = pl.multiple_of(c * tq, tq)
            kn_ref[pl.ds(r0, tq), :] = halfnorm(k_ref[pl.ds(r0, tq), :], gk_ref[...]).astype(BF16)
            return carry
        lax.fori_loop(0, seq // tq, body, 0)

    slope = slope_ref[h]
    q = halfnorm(q_ref[...], gq_ref[...]) * (DIFF_HD ** -0.5)
    q1 = jnp.where(first, q, 0.0).astype(BF16)
    q2 = jnp.where(first, 0.0, q).astype(BF16)
    row = lax.broadcasted_iota(I32, (tq, tq), 0)
    col = lax.broadcasted_iota(I32, (tq, tq), 1)
    rel = (row - col).astype(F32)
    causal = row >= col
    brel = -slope * rel

    def step(j, carry, diag):
        m1, l1, a1, m2, l2, a2 = carry
        r0 = pl.multiple_of(j * tq, tq)
        kt = kn_ref[pl.ds(r0, tq), :]
        vt = v_ref[pl.ds(r0, tq), :].astype(BF16)
        boff = -slope * ((qi - j) * tq).astype(F32)
        s1 = _qk(q1, kt) + brel + boff
        s2 = _qk(q2, kt) + brel + boff
        if diag:
            s1 = jnp.where(causal, s1, NEG_INF)
            s2 = jnp.where(causal, s2, NEG_INF)
        m1, l1, a1 = _softmax_step(s1, m1, l1, a1, vt)
        m2, l2, a2 = _softmax_step(s2, m2, l2, a2, vt)
        return m1, l1, a1, m2, l2, a2

    neg = jnp.full((tq, 1), NEG_INF, F32)
    zl = jnp.zeros((tq, 1), F32)
    za = jnp.zeros((tq, LANES), F32)
    carry = (neg, zl, za, neg, zl, za)
    carry = lax.fori_loop(0, qi, lambda j, c: step(j, c, False), carry)
    m1, l1, a1, m2, l2, a2 = step(qi, carry, True)

    lam = lam_ref[...]
    e1 = jnp.exp(jnp.sum(lam[0:1] * lam[1:2], axis=-1, keepdims=True))
    e2 = jnp.exp(jnp.sum(lam[2:3] * lam[3:4], axis=-1, keepdims=True))
    lam_full = e1 - e2 + lambda_init
    o = a1 * (1.0 / l1) - lam_full * (a2 * (1.0 / l2))
    o_ref[...] = _rms(o, gsub_ref[...]) * (1.0 - lambda_init)


def _diff_attention(z3, slopes, gq2, gk2, lam4, gsub, tq, lambda_init):
    b, s, _ = z3.shape
    kern = functools.partial(_diff_attn_kernel, tq=tq, seq=s, lambda_init=lambda_init)
    return pl.pallas_call(
        kern,
        grid=(b, DIFF_HEADS, s // tq),
        in_specs=[
            pl.BlockSpec(memory_space=pltpu.SMEM),
            pl.BlockSpec((None, tq, LANES), lambda bi, h, qi: (bi, qi, Z_DQ // LANES + h)),
            pl.BlockSpec((None, s, LANES), lambda bi, h, qi: (bi, 0, Z_DK // LANES + h)),
            pl.BlockSpec((None, s, LANES), lambda bi, h, qi: (bi, 0, Z_DV // LANES + h)),
            pl.BlockSpec((1, LANES), lambda bi, h, qi: (0, 0)),
            pl.BlockSpec((1, LANES), lambda bi, h, qi: (0, 0)),
            pl.BlockSpec((4, DIFF_HD), lambda bi, h, qi: (0, 0)),
            pl.BlockSpec((1, LANES), lambda bi, h, qi: (0, 0)),
        ],
        out_specs=pl.BlockSpec((None, tq, LANES), lambda bi, h, qi: (bi, qi, h)),
        out_shape=jax.ShapeDtypeStruct((b, s, DIFF_W), F32),
        scratch_shapes=[pltpu.VMEM((s, LANES), BF16)],
        compiler_params=_cparams(("parallel", "parallel", "arbitrary")),
        name="diff_attn",
    )(slopes, z3, z3, z3, gq2, gk2, lam4, gsub)


def _mla_proj_kernel(z_ref, gcq_ref, wuq_ref, gckv_ref, wuk_ref, wuv_ref, gq_ref, gk_ref,
                     cos_ref, sin_ref, qm_ref, km_ref, vm_ref):
    z = z_ref[...]
    cq = z[:, :Q_LORA]
    ckv = z[:, Q_LORA:Q_LORA + KV_LORA]
    krb = z[:, Q_LORA + KV_LORA:]
    cqn = _rms(cq, gcq_ref[...]).astype(BF16)
    ckvn = _rms(ckv, gckv_ref[...]).astype(BF16)
    q = jnp.dot(cqn, wuq_ref[...], preferred_element_type=F32)
    kn = jnp.dot(ckvn, wuk_ref[...], preferred_element_type=F32)
    v = jnp.dot(ckvn, wuv_ref[...], preferred_element_type=F32)
    vm_ref[...] = v.astype(BF16)

    lane = lax.broadcasted_iota(I32, (1, LANES), 1)
    in_rope = (lane >= MLA_NOPE) & (lane < MLA_QK)
    in_x1 = (lane >= MLA_NOPE) & (lane < MLA_NOPE + MLA_ROPE // 2)
    kr = jnp.where(in_rope, pltpu.roll(krb, MLA_NOPE, 1), 0.0)
    cos = cos_ref[...]
    sin = sin_ref[...]
    gq = gq_ref[...]
    gk = gk_ref[...]
    half = MLA_ROPE // 2

    def norm_rope(xh, g):
        ms = jnp.sum(xh * xh, axis=-1, keepdims=True) * (1.0 / MLA_QK)
        xn = xh * lax.rsqrt(ms + NORM_EPS) * g
        sw = jnp.where(in_x1, pltpu.roll(xn, LANES - half, 1), pltpu.roll(xn, half, 1))
        return xn * cos + sw * sin

    for hh in range(MLA_HEADS):
        sl = slice(hh * LANES, (hh + 1) * LANES)
        qm_ref[:, sl] = (norm_rope(q[:, sl], gq) * (MLA_QK ** -0.5)).astype(BF16)
        km_ref[:, sl] = norm_rope(kn[:, sl] + kr, gk).astype(BF16)


def _mla_proj(z2, gcq, wuq_p, gckv, wuk_p, wuv_p, gq_p, gk_p, cos_t, sin_t, tm, seq):
    n = z2.shape[0]
    nsb = seq // tm
    full = lambda shp: pl.BlockSpec(shp, lambda i: (0, 0))
    return pl.pallas_call(
        _mla_proj_kernel,
        grid=(n // tm,),
        in_specs=[
            pl.BlockSpec((tm, 512), lambda i: (i, Z_MLA // 512)),
            full((1, Q_LORA)), full((Q_LORA, MLA_HEADS * LANES)),
            full((1, KV_LORA)), full((KV_LORA, MLA_HEADS * LANES)), full((KV_LORA, MLA_HEADS * MLA_V)),
            full((1, LANES)), full((1, LANES)),
            pl.BlockSpec((tm, LANES), lambda i: (i % nsb, 0)),
            pl.BlockSpec((tm, LANES), lambda i: (i % nsb, 0)),
        ],
        out_specs=[
            pl.BlockSpec((tm, MLA_HEADS * LANES), lambda i: (i, 0)),
            pl.BlockSpec((tm, MLA_HEADS * LANES), lambda i: (i, 0)),
            pl.BlockSpec((tm, MLA_HEADS * MLA_V), lambda i: (i, 0)),
        ],
        out_shape=[
            jax.ShapeDtypeStruct((n, MLA_HEADS * LANES), BF16),
            jax.ShapeDtypeStruct((n, MLA_HEADS * LANES), BF16),
            jax.ShapeDtypeStruct((n, MLA_HEADS * MLA_V), BF16),
        ],
        compiler_params=_cparams(("parallel",)),
        name="mla_proj",
    )(z2, gcq, wuq_p, gckv, wuk_p, wuv_p, gq_p, gk_p, cos_t, sin_t)


def _mla_attn_kernel(q_ref, k_ref, v_ref, o_ref, *, tq):
    qi = pl.program_id(2)
    qa = q_ref[:, :LANES]
    qb = q_ref[:, LANES:]
    row = lax.broadcasted_iota(I32, (tq, tq), 0)
    col = lax.broadcasted_iota(I32, (tq, tq), 1)
    causal = row >= col

    def step(j, carry, diag):
        ma, la, aa, mb, lb, ab = carry
        r0 = pl.multiple_of(j * tq, tq)
        kt = k_ref[pl.ds(r0, tq), :]
        vt = v_ref[pl.ds(r0, tq), :]
        sa = _qk(qa, kt[:, :LANES])
        sb = _qk(qb, kt[:, LANES:])
        if diag:
            sa = jnp.where(causal, sa, NEG_INF)
            sb = jnp.where(causal, sb, NEG_INF)
        ma, la, aa = _softmax_step(sa, ma, la, aa, vt)
        mb, lb, ab = _softmax_step(sb, mb, lb, ab, vt)
        return ma, la, aa, mb, lb, ab

    neg = jnp.full((tq, 1), NEG_INF, F32)
    zl = jnp.zeros((tq, 1), F32)
    za = jnp.zeros((tq, LANES), F32)
    carry = (neg, zl, za, neg, zl, za)
    carry = lax.fori_loop(0, qi, lambda j, c: step(j, c, False), carry)
    ma, la, aa, mb, lb, ab = step(qi, carry, True)
    lane = lax.broadcasted_iota(I32, (1, LANES), 1)
    o_ref[...] = jnp.where(lane < MLA_V, aa * (1.0 / la), ab * (1.0 / lb))


def _mla_attention(qm3, km3, vm3, tq):
    b, s, _ = qm3.shape
    return pl.pallas_call(
        functools.partial(_mla_attn_kernel, tq=tq),
        grid=(b, MLA_HEADS // 2, s // tq),
        in_specs=[
            pl.BlockSpec((None, tq, 2 * LANES), lambda bi, hp, qi: (bi, qi, hp)),
            pl.BlockSpec((None, s, 2 * LANES), lambda bi, hp, qi: (bi, 0, hp)),
            pl.BlockSpec((None, s, LANES), lambda bi, hp, qi: (bi, 0, hp)),
        ],
        out_specs=pl.BlockSpec((None, tq, LANES), lambda bi, hp, qi: (bi, qi, hp)),
        out_shape=jax.ShapeDtypeStruct((b, s, MLA_HEADS * MLA_V), F32),
        compiler_params=_cparams(("parallel", "parallel", "arbitrary")),
        name="mla_attn",
    )(qm3, km3, vm3)


def _merge_kernel(x_ref, yd_ref, ym_ref, cb_ref, cc_ref, cx_ref, cch_ref, cxh_ref,
                  g0_ref, g1_ref, g2_ref, cw_ref, wd_ref, wm_ref, wc_ref, wo_ref,
                  o_ref, ucat_ref, *, tm, seq):
    i = pl.program_id(0)
    at_seq_start = (i * tm) % seq == 0
    halo = cch_ref[...] * cxh_ref[...]
    ucat_ref[0:SUBLANES, :] = jnp.where(at_seq_start, 0.0, halo)
    ucat_ref[SUBLANES:, :] = cc_ref[...] * cx_ref[...]
    cw = cw_ref[...]
    conv = (cw[0:1] * ucat_ref[SUBLANES - 2:SUBLANES - 2 + tm, :]
            + cw[1:2] * ucat_ref[SUBLANES - 1:SUBLANES - 1 + tm, :]
            + cw[2:3] * ucat_ref[SUBLANES:, :])
    yc = cb_ref[...] * conv

    def br(y, w_ref, g_ref):
        return jax.nn.sigmoid(g_ref[...]) * jnp.dot(y.astype(BF16), w_ref[...],
                                                    preferred_element_type=F32)

    mixed = br(yd_ref[...], wd_ref, g0_ref) + br(ym_ref[...], wm_ref, g1_ref) + br(yc, wc_ref, g2_ref)
    o_ref[...] = x_ref[...] + jnp.dot(mixed.astype(BF16), wo_ref[...], preferred_element_type=F32)


def _merge(x2, yd2, ym2, z2, conv_w, wd, wm, wc, wo, tm, seq):
    n = x2.shape[0]
    hb = tm // SUBLANES
    full = lambda shp: pl.BlockSpec(shp, lambda i: (0, 0))
    zc = lambda off: pl.BlockSpec((tm, 512), lambda i: (i, off // 512))
    zh = lambda off: pl.BlockSpec((SUBLANES, 512), lambda i: (jnp.maximum(i * hb - 1, 0), off // 512))
    zg = lambda k: pl.BlockSpec((tm, D_MODEL), lambda i: (i, k))
    return pl.pallas_call(
        functools.partial(_merge_kernel, tm=tm, seq=seq),
        grid=(n // tm,),
        in_specs=[
            pl.BlockSpec((tm, D_MODEL), lambda i: (i, 0)),
            pl.BlockSpec((tm, DIFF_W), lambda i: (i, 0)),
            pl.BlockSpec((tm, 512), lambda i: (i, 0)),
            zc(Z_CB), zc(Z_CC), zc(Z_CX), zh(Z_CC), zh(Z_CX),
            zg(0), zg(1), zg(2),
            full((CONV_K, CONV_W)),
            full((DIFF_W, D_MODEL)), full((512, D_MODEL)), full((CONV_W, D_MODEL)),
            full((D_MODEL, D_MODEL)),
        ],
        out_specs=pl.BlockSpec((tm, D_MODEL), lambda i: (i, 0)),
        out_shape=jax.ShapeDtypeStruct((n, D_MODEL), F32),
        scratch_shapes=[pltpu.VMEM((tm + SUBLANES, CONV_W), F32)],
        compiler_params=_cparams(("parallel",)),
        name="merge",
    )(x2, yd2, ym2, z2, z2, z2, z2, z2, z2, z2, z2, conv_w, wd, wm, wc, wo)


def _topk_rows(s, k):
    rows, t = s.shape
    riota = lax.broadcasted_iota(I32, (rows, t), 0)
    kiota = lax.broadcasted_iota(I32, (k, t), 0)
    vals = jnp.zeros((k, t), F32)
    idxs = jnp.zeros((k, t), I32)
    for i in range(k):
        m = jnp.max(s, axis=0, keepdims=True)
        am = jnp.min(jnp.where(s == m, riota, rows), axis=0, keepdims=True)
        vals = jnp.where(kiota == i, m, vals)
        idxs = jnp.where(kiota == i, am, idxs)
        s = jnp.where(riota == am, -jnp.inf, s)
    return vals, idxs


def _route_kernel(x_ref, g_ref, wq_ref, keys_ref, h_ref, exp_ref, gate_ref,
                  q_ref, expt_ref, gatet_ref, *, tm):
    h2 = _rms(x_ref[...], g_ref[...])
    h_ref[...] = h2
    q_ref[...] = jnp.dot(h2.astype(BF16), wq_ref[...], preferred_element_type=F32)

    def head(hh, carry):
        tops = []
        for p in range(2):
            c0 = pl.multiple_of(hh * (2 * PEER_DK) + p * PEER_DK, PEER_DK)
            qhp = q_ref[:, pl.ds(c0, PEER_DK)].astype(BF16)
            keys = keys_ref[p, hh]
            st = _qk(keys, qhp)
            tops.append(_topk_rows(st, K_SUB))
        (s0, i0), (s1, i1) = tops
        cand = jnp.concatenate([s0[a:a + 1] + s1 for a in range(K_SUB)], axis=0)
        cexp = jnp.concatenate([i0[a:a + 1] * N_KEYS + i1 for a in range(K_SUB)], axis=0)
        nc = K_SUB * K_SUB
        riota = lax.broadcasted_iota(I32, (nc, tm), 0)
        kiota = lax.broadcasted_iota(I32, (PEER_TOPK, tm), 0)
        best = jnp.zeros((PEER_TOPK, tm), F32)
        bexp = jnp.zeros((PEER_TOPK, tm), I32)
        for i in range(PEER_TOPK):
            m = jnp.max(cand, axis=0, keepdims=True)
            am = jnp.min(jnp.where(cand == m, riota, nc), axis=0, keepdims=True)
            hit = riota == am
            e = jnp.max(jnp.where(hit, cexp, -1), axis=0, keepdims=True)
            best = jnp.where(kiota == i, m, best)
            bexp = jnp.where(kiota == i, e, bexp)
            cand = jnp.where(hit, -jnp.inf, cand)
        ex = jnp.exp(best - best[0:1])
        gate = ex / jnp.sum(ex, axis=0, keepdims=True)
        r0 = pl.multiple_of(hh * PEER_TOPK, PEER_TOPK)
        expt_ref[pl.ds(r0, PEER_TOPK), :] = bexp
        gatet_ref[pl.ds(r0, PEER_TOPK), :] = gate
        return carry

    lax.fori_loop(0, PEER_HEADS, head, 0)
    exp_ref[...] = expt_ref[...].T
    gate_ref[...] = gatet_ref[...].T


def _route(x2, g, wq, keys, tm):
    n = x2.shape[0]
    return pl.pallas_call(
        functools.partial(_route_kernel, tm=tm),
        grid=(n // tm,),
        in_specs=[
            pl.BlockSpec((tm, D_MODEL), lambda i: (i, 0)),
            pl.BlockSpec((1, D_MODEL), lambda i: (0, 0)),
            pl.BlockSpec((D_MODEL, 2 * PEER_DK * PEER_HEADS), lambda i: (0, 0)),
            pl.BlockSpec((2, PEER_HEADS, N_KEYS, PEER_DK), lambda i: (0, 0, 0, 0)),
        ],
        out_specs=[
            pl.BlockSpec((tm, D_MODEL), lambda i: (i, 0)),
            pl.BlockSpec((tm, PEER_PAIRS), lambda i: (i, 0)),
            pl.BlockSpec((tm, PEER_PAIRS), lambda i: (i, 0)),
        ],
        out_shape=[
            jax.ShapeDtypeStruct((n, D_MODEL), F32),
            jax.ShapeDtypeStruct((n, PEER_PAIRS), I32),
            jax.ShapeDtypeStruct((n, PEER_PAIRS), F32),
        ],
        scratch_shapes=[
            pltpu.VMEM((tm, 2 * PEER_DK * PEER_HEADS), F32),
            pltpu.VMEM((PEER_PAIRS, tm), I32),
            pltpu.VMEM((PEER_PAIRS, tm), F32),
        ],
        compiler_params=_cparams(("parallel",)),
        name="peer_route",
    )(x2, g, wq, keys)


HALF_EXPERTS = N_EXPERTS // 2
HI_MASK = np.uint32(0xFFFF0000)


def _expert_row(tab_ref, e):
    tile = tab_ref[e & (HALF_EXPERTS - 1)]
    sh = ((1 - (e >> 13)) * 16).astype(jnp.uint32)
    return pltpu.bitcast((tile << sh) & HI_MASK, F32)


def _load_table_once(tab_hbm, tab_ref, sem):
    @pl.when(pl.program_id(0) == 0)
    def _():
        cp = pltpu.make_async_copy(tab_hbm, tab_ref, sem)
        cp.start()
        cp.wait()


def _peer_u_kernel(idx_ref, tab_hbm, h_ref, gate_ref, w_ref, tab_ref, sem, *, tb):
    _load_table_once(tab_hbm, tab_ref, sem)
    lane = lax.broadcasted_iota(I32, (SUBLANES, LANES), 1)

    def tok(t, carry):
        hv = h_ref[t]
        z = jnp.zeros((SUBLANES, LANES), F32)
        for k in range(PEER_PAIRS):
            row = _expert_row(tab_ref, idx_ref[t, k])
            r = jnp.sum(row * hv, axis=1, keepdims=True)
            z = jnp.where(lane == k, r, z)
        act = jnp.sum(z, axis=0, keepdims=True)
        gelu = 0.5 * act * (1.0 + lax.erf(act * (2.0 ** -0.5)))
        w_ref[pl.ds(t, 1), :] = gate_ref[pl.ds(t, 1), :] * gelu
        return carry

    lax.fori_loop(0, tb, tok, 0)


def _peer_u(idx, tab, h3, gate, tb):
    n = idx.shape[0]
    return pl.pallas_call(
        functools.partial(_peer_u_kernel, tb=tb),
        grid=(n // tb,),
        in_specs=[
            pl.BlockSpec((tb, PEER_PAIRS), lambda i: (i, 0), memory_space=pltpu.SMEM),
            pl.BlockSpec(memory_space=pl.ANY),
            pl.BlockSpec((tb, SUBLANES, LANES), lambda i: (i, 0, 0)),
            pl.BlockSpec((tb, PEER_PAIRS), lambda i: (i, 0)),
        ],
        out_specs=pl.BlockSpec((tb, PEER_PAIRS), lambda i: (i, 0)),
        out_shape=jax.ShapeDtypeStruct((n, PEER_PAIRS), F32),
        scratch_shapes=[
            pltpu.VMEM((HALF_EXPERTS, SUBLANES, LANES), jnp.uint32),
            pltpu.SemaphoreType.DMA(()),
        ],
        compiler_params=_cparams(("arbitrary",)),
        name="peer_u",
    )(idx, tab, h3, gate)


def _peer_v_kernel(idx_ref, w_ref, tab_hbm, o_ref, tab_ref, sem, *, tb):
    _load_table_once(tab_hbm, tab_ref, sem)

    def tok(t, carry):
        acc = jnp.zeros((SUBLANES, LANES), F32)
        for k in range(PEER_PAIRS):
            acc = acc + w_ref[t, k] * _expert_row(tab_ref, idx_ref[t, k])
        o_ref[t] = acc
        return carry

    lax.fori_loop(0, tb, tok, 0)


def _peer_v(idx, w, tab, tb):
    n = idx.shape[0]
    return pl.pallas_call(
        functools.partial(_peer_v_kernel, tb=tb),
        grid=(n // tb,),
        in_specs=[
            pl.BlockSpec((tb, PEER_PAIRS), lambda i: (i, 0), memory_space=pltpu.SMEM),
            pl.BlockSpec((tb, PEER_PAIRS), lambda i: (i, 0), memory_space=pltpu.SMEM),
            pl.BlockSpec(memory_space=pl.ANY),
        ],
        out_specs=pl.BlockSpec((tb, SUBLANES, LANES), lambda i: (i, 0, 0)),
        out_shape=jax.ShapeDtypeStruct((n, SUBLANES, LANES), F32),
        scratch_shapes=[
            pltpu.VMEM((HALF_EXPERTS, SUBLANES, LANES), jnp.uint32),
            pltpu.SemaphoreType.DMA(()),
        ],
        compiler_params=_cparams(("arbitrary",)),
        name="peer_v",
    )(idx, w, tab)


def _ple_kernel(x_ref, peer_ref, p_ref, g_ref, wg_ref, wp_ref, o_ref):
    x = x_ref[...] + peer_ref[...]
    gate = jax.nn.sigmoid(jnp.dot(_rms(x, g_ref[...]).astype(BF16), wg_ref[...],
                                  preferred_element_type=F32))
    emb = jnp.dot(p_ref[...].astype(BF16), wp_ref[...], preferred_element_type=F32)
    o_ref[...] = x + gate * emb


def _ple(x2, peer2, p2, g, wg, wp, tm):
    n = x2.shape[0]
    full = lambda shp: pl.BlockSpec(shp, lambda i: (0, 0))
    return pl.pallas_call(
        _ple_kernel,
        grid=(n // tm,),
        in_specs=[
            pl.BlockSpec((tm, D_MODEL), lambda i: (i, 0)),
            pl.BlockSpec((tm, D_MODEL), lambda i: (i, 0)),
            pl.BlockSpec((tm, PLE_DIM), lambda i: (i, 0)),
            full((1, D_MODEL)), full((D_MODEL, D_MODEL)), full((PLE_DIM, D_MODEL)),
        ],
        out_specs=pl.BlockSpec((tm, D_MODEL), lambda i: (i, 0)),
        out_shape=jax.ShapeDtypeStruct((n, D_MODEL), F32),
        compiler_params=_cparams(("parallel",)),
        name="ple",
    )(x2, peer2, p2, g, wg, wp)


def _relayout_w_in(w_in):
    o = np.cumsum([0, 512, 512, 512, Q_LORA, KV_LORA, MLA_ROPE, 512, 512, 512, 3 * D_MODEL])
    seg = lambda k: w_in[:, o[k]:o[k + 1]]
    pad = jnp.zeros((D_MODEL, 512 - Q_LORA - KV_LORA - MLA_ROPE), w_in.dtype)
    return jnp.concatenate([seg(9), seg(0), seg(1), seg(2), seg(6), seg(7), seg(8),
                            seg(3), seg(4), seg(5), pad], axis=1).astype(BF16)


def _pad_heads(w, width):
    k = w.shape[0]
    w3 = w.reshape(k, MLA_HEADS, width)
    return jnp.pad(w3, ((0, 0), (0, 0), (0, LANES - width))).reshape(k, MLA_HEADS * LANES)


def _pack_table(tab):
    bits = lax.bitcast_convert_type(tab.astype(BF16), jnp.uint16).astype(jnp.uint32)
    packed = bits[:HALF_EXPERTS] | (bits[HALF_EXPERTS:] << 16)
    return packed.reshape(HALF_EXPERTS, SUBLANES, LANES)


def _rope_tables(seq):
    half = MLA_ROPE // 2
    freqs = ROPE_THETA ** (-np.arange(half, dtype=np.float32) / half)
    ang = np.arange(seq, dtype=np.float32)[:, None] * freqs[None, :]
    cos = np.ones((seq, LANES), np.float32)
    sin = np.zeros((seq, LANES), np.float32)
    cos[:, MLA_NOPE:MLA_NOPE + half] = np.cos(ang)
    cos[:, MLA_NOPE + half:MLA_QK] = np.cos(ang)
    sin[:, MLA_NOPE:MLA_NOPE + half] = -np.sin(ang)
    sin[:, MLA_NOPE + half:MLA_QK] = np.sin(ang)
    return jnp.asarray(cos), jnp.asarray(sin)


def _tile(n, pref):
    return pref if n % pref == 0 else n


def _layer(x2, p2, layer_idx, batch, seq, g_mix, w_in, g_diff_q, g_diff_k, lam_q1, lam_k1, lam_q2,
           lam_k2, g_diff_sub, g_cq, w_uq, g_ckv, w_ukv, g_mla_q, g_mla_k, conv_w, w_br_diff,
           w_br_mla, w_br_conv, w_out, g_ffn, w_query, sub_keys, u_experts, v_experts, g_ple,
           w_ple_gate, w_ple_proj):
    n = x2.shape[0]
    lambda_init = 0.8 - 0.6 * math.exp(-0.3 * layer_idx)
    tm = _tile(seq, 256)
    tq = _tile(seq, 512)
    row = lambda v: v.reshape(1, -1)

    z2 = _inproj(x2, row(g_mix), _relayout_w_in(w_in), tm)
    z3 = z2.reshape(batch, seq, Z_W)

    slopes = jnp.asarray(2.0 ** (-8.0 * np.arange(1, DIFF_HEADS + 1, dtype=np.float32) / DIFF_HEADS))
    gq2 = jnp.tile(g_diff_q, 2).reshape(1, LANES)
    gk2 = jnp.tile(g_diff_k, 2).reshape(1, LANES)
    lam4 = jnp.stack([lam_q1, lam_k1, lam_q2, lam_k2])
    yd = _diff_attention(z3, slopes, gq2, gk2, lam4, row(g_diff_sub), tq, lambda_init)

    w_ukv3 = w_ukv.reshape(KV_LORA, MLA_HEADS, MLA_NOPE + MLA_V)
    wuk_p = _pad_heads(w_ukv3[:, :, :MLA_NOPE].reshape(KV_LORA, -1), MLA_NOPE).astype(BF16)
    wuv_p = w_ukv3[:, :, MLA_NOPE:].reshape(KV_LORA, -1).astype(BF16)
    wuq_p = _pad_heads(w_uq, MLA_QK).astype(BF16)
    gq_p = jnp.pad(g_mla_q, (0, LANES - MLA_QK)).reshape(1, LANES)
    gk_p = jnp.pad(g_mla_k, (0, LANES - MLA_QK)).reshape(1, LANES)
    cos_t, sin_t = _rope_tables(seq)
    qm, km, vm = _mla_proj(z2, row(g_cq), wuq_p, row(g_ckv), wuk_p, wuv_p, gq_p, gk_p,
                           cos_t, sin_t, tm, seq)
    ym = _mla_attention(qm.reshape(batch, seq, -1), km.reshape(batch, seq, -1),
                        vm.reshape(batch, seq, -1), tq)

    x2 = _merge(x2, yd.reshape(n, DIFF_W), ym.reshape(n, -1), z2, conv_w,
                w_br_diff.astype(BF16), w_br_mla.astype(BF16), w_br_conv.astype(BF16),
                w_out.astype(BF16), tm, seq)

    h2, expert, gate = _route(x2, row(g_ffn), w_query.astype(BF16), sub_keys.astype(BF16), tm)
    tb = _tile(n, 64)
    w = _peer_u(expert, _pack_table(u_experts), h2.reshape(n, SUBLANES, LANES), gate, tb)
    peer = _peer_v(expert, w, _pack_table(v_experts), tb)

    return _ple(x2, peer.reshape(n, D_MODEL), p2, row(g_ple), w_ple_gate.astype(BF16),
                w_ple_proj.astype(BF16), tm)


def kernel(x, p, g_mix, w_in, g_diff_q, g_diff_k, lam_q1, lam_k1, lam_q2, lam_k2, g_diff_sub, g_cq,
           w_uq, g_ckv, w_ukv, g_mla_q, g_mla_k, conv_w, w_br_diff, w_br_mla, w_br_conv, w_out,
           g_ffn, w_query, sub_keys, u_experts, v_experts, g_ple, w_ple_gate, w_ple_proj):
    batch, seq, _ = x.shape
    n = batch * seq
    x2 = x.reshape(n, D_MODEL)
    per_layer = (g_mix, w_in, g_diff_q, g_diff_k, lam_q1, lam_k1, lam_q2, lam_k2, g_diff_sub, g_cq,
                 w_uq, g_ckv, w_ukv, g_mla_q, g_mla_k, conv_w, w_br_diff, w_br_mla, w_br_conv,
                 w_out, g_ffn, w_query, sub_keys, u_experts, v_experts, g_ple, w_ple_gate,
                 w_ple_proj)
    for i in range(p.shape[0]):
        x2 = _layer(x2, p[i].reshape(n, PLE_DIM), i, batch, seq, *(a[i] for a in per_layer))
    return x2.reshape(batch, seq, D_MODEL)
```

```python
import functools
import math

import jax
import jax.numpy as jnp
import numpy as np
from jax import lax
from jax.experimental import pallas as pl
from jax.experimental.pallas import tpu as pltpu

F32 = jnp.float32
BF16 = jnp.bfloat16
I32 = jnp.int32

D_MODEL = 1024
DIFF_HEADS = 4
DIFF_HD = 64
DIFF_W = 512
MLA_HEADS = 8
MLA_NOPE = 64
MLA_ROPE = 32
MLA_V = 64
MLA_QK = 96
Q_LORA = 256
KV_LORA = 128
ROPE_THETA = 10000.0
CONV_W = 512
CONV_K = 3
PEER_HEADS = 8
N_KEYS = 128
N_EXPERTS = N_KEYS * N_KEYS
PEER_DK = 128
K_SUB = 16
PEER_TOPK = 16
PEER_PAIRS = PEER_HEADS * PEER_TOPK
PLE_DIM = 256
NORM_EPS = 1e-6
NEG_INF = -1e30

LANES = 128
SUBLANES = 8
VMEM_LIMIT = 56 * 1024 * 1024

Z_GATES = 0
Z_DQ = 3072
Z_DK = 3584
Z_DV = 4096
Z_CB = 4608
Z_CC = 5120
Z_CX = 5632
Z_MLA = 6144
Z_W = 6656


def _cparams(sem, vmem=VMEM_LIMIT):
    return pltpu.CompilerParams(dimension_semantics=sem, vmem_limit_bytes=vmem)


def _rms(x, g):
    return x * lax.rsqrt(jnp.mean(x * x, axis=-1, keepdims=True) + NORM_EPS) * g


def _inproj_kernel(x_ref, g_ref, w_ref, z_ref):
    h = _rms(x_ref[...], g_ref[...]).astype(BF16)
    z_ref[...] = jnp.dot(h, w_ref[...], preferred_element_type=F32)


def _inproj(x2, g, w_in_r, tm):
    n = x2.shape[0]
    return pl.pallas_call(
        _inproj_kernel,
        grid=(n // tm,),
        in_specs=[
            pl.BlockSpec((tm, D_MODEL), lambda i: (i, 0)),
            pl.BlockSpec((1, D_MODEL), lambda i: (0, 0)),
            pl.BlockSpec((D_MODEL, Z_W), lambda i: (0, 0)),
        ],
        out_specs=pl.BlockSpec((tm, Z_W), lambda i: (i, 0)),
        out_shape=jax.ShapeDtypeStruct((n, Z_W), F32),
        compiler_params=_cparams(("parallel",)),
        name="inproj",
    )(x2, g, w_in_r)


def _softmax_step(s, m, l, a, vt):
    mn = jnp.maximum(m, jnp.max(s, axis=-1, keepdims=True))
    p = jnp.exp(s - mn)
    alpha = jnp.exp(m - mn)
    l = alpha * l + jnp.sum(p, axis=-1, keepdims=True)
    a = alpha * a + jnp.dot(p.astype(BF16), vt, preferred_element_type=F32)
    return mn, l, a


def _qk(q, kt):
    return lax.dot_general(q, kt, (((1,), (1,)), ((), ())), preferred_element_type=F32)


def _diff_attn_kernel(slope_ref, q_ref, k_ref, v_ref, gq_ref, gk_ref, lam_ref, gsub_ref,
                      o_ref, kn_ref, *, tq, seq, lambda_init):
    h = pl.program_id(1)
    qi = pl.program_id(2)
    lane = lax.broadcasted_iota(I32, (1, LANES), 1)
    first = lane < DIFF_HD

    def halfnorm(x, g):
        x2 = x * x
        s_all = jnp.sum(x2, axis=-1, keepdims=True)
        s_lo = jnp.sum(jnp.where(first, x2, 0.0), axis=-1, keepdims=True)
        ms = jnp.where(first, s_lo, s_all - s_lo) * (1.0 / DIFF_HD)
        return x * lax.rsqrt(ms + NORM_EPS) * g

    @pl.when(qi == 0)
    def _():
        def body(c, carry):
            r0 = pl.multiple_of(c * tq, tq)
            kn_ref[pl.ds(r0, tq), :] = halfnorm(k_ref[pl.ds(r0, tq), :], gk_ref[...]).astype(BF16)
            return carry
        lax.fori_loop(0, seq // tq, body, 0)

    slope = slope_ref[h]
    q = halfnorm(q_ref[...], gq_ref[...]) * (DIFF_HD ** -0.5)
    q1 = jnp.where(first, q, 0.0).astype(BF16)
    q2 = jnp.where(first, 0.0, q).astype(BF16)
    row = lax.broadcasted_iota(I32, (tq, tq), 0)
    col = lax.broadcasted_iota(I32, (tq, tq), 1)
    rel = (row - col).astype(F32)
    causal = row >= col
    brel = -slope * rel

    def step(j, carry, diag):
        m1, l1, a1, m2, l2, a2 = carry
        r0 = pl.multiple_of(j * tq, tq)
        kt = kn_ref[pl.ds(r0, tq), :]
        vt = v_ref[pl.ds(r0, tq), :].astype(BF16)
        boff = -slope * ((qi - j) * tq).astype(F32)
        s1 = _qk(q1, kt) + brel + boff
        s2 = _qk(q2, kt) + brel + boff
        if diag:
            s1 = jnp.where(causal, s1, NEG_INF)
            s2 = jnp.where(causal, s2, NEG_INF)
        m1, l1, a1 = _softmax_step(s1, m1, l1, a1, vt)
        m2, l2, a2 = _softmax_step(s2, m2, l2, a2, vt)
        return m1, l1, a1, m2, l2, a2

    neg = jnp.full((tq, 1), NEG_INF, F32)
    zl = jnp.zeros((tq, 1), F32)
    za = jnp.zeros((tq, LANES), F32)
    carry = (neg, zl, za, neg, zl, za)
    carry = lax.fori_loop(0, qi, lambda j, c: step(j, c, False), carry)
    m1, l1, a1, m2, l2, a2 = step(qi, carry, True)

    lam = lam_ref[...]
    e1 = jnp.exp(jnp.sum(lam[0:1] * lam[1:2], axis=-1, keepdims=True))
    e2 = jnp.exp(jnp.sum(lam[2:3] * lam[3:4], axis=-1, keepdims=True))
    lam_full = e1 - e2 + lambda_init
    o = a1 * (1.0 / l1) - lam_full * (a2 * (1.0 / l2))
    o_ref[...] = _rms(o, gsub_ref[...]) * (1.0 - lambda_init)


def _diff_attention(z3, slopes, gq2, gk2, lam4, gsub, tq, lambda_init):
    b, s, _ = z3.shape
    kern = functools.partial(_diff_attn_kernel, tq=tq, seq=s, lambda_init=lambda_init)
    return pl.pallas_call(
        kern,
        grid=(b, DIFF_HEADS, s // tq),
        in_specs=[
            pl.BlockSpec(memory_space=pltpu.SMEM),
            pl.BlockSpec((None, tq, LANES), lambda bi, h, qi: (bi, qi, Z_DQ // LANES + h)),
            pl.BlockSpec((None, s, LANES), lambda bi, h, qi: (bi, 0, Z_DK // LANES + h)),
            pl.BlockSpec((None, s, LANES), lambda bi, h, qi: (bi, 0, Z_DV // LANES + h)),
            pl.BlockSpec((1, LANES), lambda bi, h, qi: (0, 0)),
            pl.BlockSpec((1, LANES), lambda bi, h, qi: (0, 0)),
            pl.BlockSpec((4, DIFF_HD), lambda bi, h, qi: (0, 0)),
            pl.BlockSpec((1, LANES), lambda bi, h, qi: (0, 0)),
        ],
        out_specs=pl.BlockSpec((None, tq, LANES), lambda bi, h, qi: (bi, qi, h)),
        out_shape=jax.ShapeDtypeStruct((b, s, DIFF_W), F32),
        scratch_shapes=[pltpu.VMEM((s, LANES), BF16)],
        compiler_params=_cparams(("parallel", "parallel", "arbitrary")),
        name="diff_attn",
    )(slopes, z3, z3, z3, gq2, gk2, lam4, gsub)


def _mla_proj_kernel(z_ref, gcq_ref, wuq_ref, gckv_ref, wuk_ref, wuv_ref, gq_ref, gk_ref,
                     cos_ref, sin_ref, qm_ref, km_ref, vm_ref):
    z = z_ref[...]
    cq = z[:, :Q_LORA]
    ckv = z[:, Q_LORA:Q_LORA + KV_LORA]
    krb = z[:, Q_LORA + KV_LORA:]
    cqn = _rms(cq, gcq_ref[...]).astype(BF16)
    ckvn = _rms(ckv, gckv_ref[...]).astype(BF16)
    q = jnp.dot(cqn, wuq_ref[...], preferred_element_type=F32)
    kn = jnp.dot(ckvn, wuk_ref[...], preferred_element_type=F32)
    v = jnp.dot(ckvn, wuv_ref[...], preferred_element_type=F32)
    vm_ref[...] = v.astype(BF16)

    lane = lax.broadcasted_iota(I32, (1, LANES), 1)
    in_rope = (lane >= MLA_NOPE) & (lane < MLA_QK)
    in_x1 = (lane >= MLA_NOPE) & (lane < MLA_NOPE + MLA_ROPE // 2)
    kr = jnp.where(in_rope, pltpu.roll(krb, MLA_NOPE, 1), 0.0)
    cos = cos_ref[...]
    sin = sin_ref[...]
    gq = gq_ref[...]
    gk = gk_ref[...]
    half = MLA_ROPE // 2

    def norm_rope(xh, g):
        ms = jnp.sum(xh * xh, axis=-1, keepdims=True) * (1.0 / MLA_QK)
        xn = xh * lax.rsqrt(ms + NORM_EPS) * g
        sw = jnp.where(in_x1, pltpu.roll(xn, LANES - half, 1), pltpu.roll(xn, half, 1))
        return xn * cos + sw * sin

    for hh in range(MLA_HEADS):
        sl = slice(hh * LANES, (hh + 1) * LANES)
        qm_ref[:, sl] = (norm_rope(q[:, sl], gq) * (MLA_QK ** -0.5)).astype(BF16)
        km_ref[:, sl] = norm_rope(kn[:, sl] + kr, gk).astype(BF16)


def _mla_proj(z2, gcq, wuq_p, gckv, wuk_p, wuv_p, gq_p, gk_p, cos_t, sin_t, tm, seq):
    n = z2.shape[0]
    nsb = seq // tm
    full = lambda shp: pl.BlockSpec(shp, lambda i: (0, 0))
    return pl.pallas_call(
        _mla_proj_kernel,
        grid=(n // tm,),
        in_specs=[
            pl.BlockSpec((tm, 512), lambda i: (i, Z_MLA // 512)),
            full((1, Q_LORA)), full((Q_LORA, MLA_HEADS * LANES)),
            full((1, KV_LORA)), full((KV_LORA, MLA_HEADS * LANES)), full((KV_LORA, MLA_HEADS * MLA_V)),
            full((1, LANES)), full((1, LANES)),
            pl.BlockSpec((tm, LANES), lambda i: (i % nsb, 0)),
            pl.BlockSpec((tm, LANES), lambda i: (i % nsb, 0)),
        ],
        out_specs=[
            pl.BlockSpec((tm, MLA_HEADS * LANES), lambda i: (i, 0)),
            pl.BlockSpec((tm, MLA_HEADS * LANES), lambda i: (i, 0)),
            pl.BlockSpec((tm, MLA_HEADS * MLA_V), lambda i: (i, 0)),
        ],
        out_shape=[
            jax.ShapeDtypeStruct((n, MLA_HEADS * LANES), BF16),
            jax.ShapeDtypeStruct((n, MLA_HEADS * LANES), BF16),
            jax.ShapeDtypeStruct((n, MLA_HEADS * MLA_V), BF16),
        ],
        compiler_params=_cparams(("parallel",)),
        name="mla_proj",
    )(z2, gcq, wuq_p, gckv, wuk_p, wuv_p, gq_p, gk_p, cos_t, sin_t)


def _mla_attn_kernel(q_ref, k_ref, v_ref, o_ref, *, tq):
    qi = pl.program_id(2)
    qa = q_ref[:, :LANES]
    qb = q_ref[:, LANES:]
    row = lax.broadcasted_iota(I32, (tq, tq), 0)
    col = lax.broadcasted_iota(I32, (tq, tq), 1)
    causal = row >= col

    def step(j, carry, diag):
        ma, la, aa, mb, lb, ab = carry
        r0 = pl.multiple_of(j * tq, tq)
        kt = k_ref[pl.ds(r0, tq), :]
        vt = v_ref[pl.ds(r0, tq), :]
        sa = _qk(qa, kt[:, :LANES])
        sb = _qk(qb, kt[:, LANES:])
        if diag:
            sa = jnp.where(causal, sa, NEG_INF)
            sb = jnp.where(causal, sb, NEG_INF)
        ma, la, aa = _softmax_step(sa, ma, la, aa, vt)
        mb, lb, ab = _softmax_step(sb, mb, lb, ab, vt)
        return ma, la, aa, mb, lb, ab

    neg = jnp.full((tq, 1), NEG_INF, F32)
    zl = jnp.zeros((tq, 1), F32)
    za = jnp.zeros((tq, LANES), F32)
    carry = (neg, zl, za, neg, zl, za)
    carry = lax.fori_loop(0, qi, lambda j, c: step(j, c, False), carry)
    ma, la, aa, mb, lb, ab = step(qi, carry, True)
    lane = lax.broadcasted_iota(I32, (1, LANES), 1)
    o_ref[...] = jnp.where(lane < MLA_V, aa * (1.0 / la), ab * (1.0 / lb))


def _mla_attention(qm3, km3, vm3, tq):
    b, s, _ = qm3.shape
    return pl.pallas_call(
        functools.partial(_mla_attn_kernel, tq=tq),
        grid=(b, MLA_HEADS // 2, s // tq),
        in_specs=[
            pl.BlockSpec((None, tq, 2 * LANES), lambda bi, hp, qi: (bi, qi, hp)),
            pl.BlockSpec((None, s, 2 * LANES), lambda bi, hp, qi: (bi, 0, hp)),
            pl.BlockSpec((None, s, LANES), lambda bi, hp, qi: (bi, 0, hp)),
        ],
        out_specs=pl.BlockSpec((None, tq, LANES), lambda bi, hp, qi: (bi, qi, hp)),
        out_shape=jax.ShapeDtypeStruct((b, s, MLA_HEADS * MLA_V), F32),
        compiler_params=_cparams(("parallel", "parallel", "arbitrary")),
        name="mla_attn",
    )(qm3, km3, vm3)


def _merge_kernel(x_ref, yd_ref, ym_ref, cb_ref, cc_ref, cx_ref, cch_ref, cxh_ref,
                  g0_ref, g1_ref, g2_ref, cw_ref, wd_ref, wm_ref, wc_ref, wo_ref,
                  o_ref, ucat_ref, *, tm, seq):
    i = pl.program_id(0)
    at_seq_start = (i * tm) % seq == 0
    halo = cch_ref[...] * cxh_ref[...]
    ucat_ref[0:SUBLANES, :] = jnp.where(at_seq_start, 0.0, halo)
    ucat_ref[SUBLANES:, :] = cc_ref[...] * cx_ref[...]
    cw = cw_ref[...]
    conv = (cw[0:1] * ucat_ref[SUBLANES - 2:SUBLANES - 2 + tm, :]
            + cw[1:2] * ucat_ref[SUBLANES - 1:SUBLANES - 1 + tm, :]
            + cw[2:3] * ucat_ref[SUBLANES:, :])
    yc = cb_ref[...] * conv

    def br(y, w_ref, g_ref):
        return jax.nn.sigmoid(g_ref[...]) * jnp.dot(y.astype(BF16), w_ref[...],
                                                    preferred_element_type=F32)

    mixed = br(yd_ref[...], wd_ref, g0_ref) + br(ym_ref[...], wm_ref, g1_ref) + br(yc, wc_ref, g2_ref)
    o_ref[...] = x_ref[...] + jnp.dot(mixed.astype(BF16), wo_ref[...], preferred_element_type=F32)


def _merge(x2, yd2, ym2, z2, conv_w, wd, wm, wc, wo, tm, seq):
    n = x2.shape[0]
    hb = tm // SUBLANES
    full = lambda shp: pl.BlockSpec(shp, lambda i: (0, 0))
    zc = lambda off: pl.BlockSpec((tm, 512), lambda i: (i, off // 512))
    zh = lambda off: pl.BlockSpec((SUBLANES, 512), lambda i: (jnp.maximum(i * hb - 1, 0), off // 512))
    zg = lambda k: pl.BlockSpec((tm, D_MODEL), lambda i: (i, k))
    return pl.pallas_call(
        functools.partial(_merge_kernel, tm=tm, seq=seq),
        grid=(n // tm,),
        in_specs=[
            pl.BlockSpec((tm, D_MODEL), lambda i: (i, 0)),
            pl.BlockSpec((tm, DIFF_W), lambda i: (i, 0)),
            pl.BlockSpec((tm, 512), lambda i: (i, 0)),
            zc(Z_CB), zc(Z_CC), zc(Z_CX), zh(Z_CC), zh(Z_CX),
            zg(0), zg(1), zg(2),
            full((CONV_K, CONV_W)),
            full((DIFF_W, D_MODEL)), full((512, D_MODEL)), full((CONV_W, D_MODEL)),
            full((D_MODEL, D_MODEL)),
        ],
        out_specs=pl.BlockSpec((tm, D_MODEL), lambda i: (i, 0)),
        out_shape=jax.ShapeDtypeStruct((n, D_MODEL), F32),
        scratch_shapes=[pltpu.VMEM((tm + SUBLANES, CONV_W), F32)],
        compiler_params=_cparams(("parallel",)),
        name="merge",
    )(x2, yd2, ym2, z2, z2, z2, z2, z2, z2, z2, z2, conv_w, wd, wm, wc, wo)


def _topk_rows(s, k):
    rows, t = s.shape
    riota = lax.broadcasted_iota(I32, (rows, t), 0)
    kiota = lax.broadcasted_iota(I32, (k, t), 0)
    vals = jnp.zeros((k, t), F32)
    idxs = jnp.zeros((k, t), I32)
    for i in range(k):
        m = jnp.max(s, axis=0, keepdims=True)
        am = jnp.min(jnp.where(s == m, riota, rows), axis=0, keepdims=True)
        vals = jnp.where(kiota == i, m, vals)
        idxs = jnp.where(kiota == i, am, idxs)
        s = jnp.where(riota == am, -jnp.inf, s)
    return vals, idxs


def _route_kernel(x_ref, g_ref, wq_ref, keys_ref, h_ref, idx_ref, hi_ref, gate_ref,
                  q_ref, expt_ref, gatet_ref, *, tm):
    h2 = _rms(x_ref[...], g_ref[...])
    h_ref[...] = h2
    q_ref[...] = jnp.dot(h2.astype(BF16), wq_ref[...], preferred_element_type=F32)

    def head(hh, carry):
        tops = []
        for p in range(2):
            c0 = pl.multiple_of(hh * (2 * PEER_DK) + p * PEER_DK, PEER_DK)
            qhp = q_ref[:, pl.ds(c0, PEER_DK)].astype(BF16)
            keys = keys_ref[p, hh]
            st = _qk(keys, qhp)
            tops.append(_topk_rows(st, K_SUB))
        (s0, i0), (s1, i1) = tops
        cand = jnp.concatenate([s0[a:a + 1] + s1 for a in range(K_SUB)], axis=0)
        cexp = jnp.concatenate([i0[a:a + 1] * N_KEYS + i1 for a in range(K_SUB)], axis=0)
        nc = K_SUB * K_SUB
        riota = lax.broadcasted_iota(I32, (nc, tm), 0)
        kiota = lax.broadcasted_iota(I32, (PEER_TOPK, tm), 0)
        best = jnp.zeros((PEER_TOPK, tm), F32)
        bexp = jnp.zeros((PEER_TOPK, tm), I32)
        for i in range(PEER_TOPK):
            m = jnp.max(cand, axis=0, keepdims=True)
            am = jnp.min(jnp.where(cand == m, riota, nc), axis=0, keepdims=True)
            hit = riota == am
            e = jnp.max(jnp.where(hit, cexp, -1), axis=0, keepdims=True)
            best = jnp.where(kiota == i, m, best)
            bexp = jnp.where(kiota == i, e, bexp)
            cand = jnp.where(hit, -jnp.inf, cand)
        ex = jnp.exp(best - best[0:1])
        gate = ex / jnp.sum(ex, axis=0, keepdims=True)
        r0 = pl.multiple_of(hh * PEER_TOPK, PEER_TOPK)
        expt_ref[pl.ds(r0, PEER_TOPK), :] = bexp
        gatet_ref[pl.ds(r0, PEER_TOPK), :] = gate
        return carry

    lax.fori_loop(0, PEER_HEADS, head, 0)
    expert = expt_ref[...].T
    idx_ref[...] = (expert & (N_EXPERTS // 2 - 1)) * SUBLANES
    hi_ref[...] = (expert >> 13).astype(F32).astype(BF16)
    gate_ref[...] = gatet_ref[...].T


def _route(x2, g, wq, keys, tm):
    n = x2.shape[0]
    return pl.pallas_call(
        functools.partial(_route_kernel, tm=tm),
        grid=(n // tm,),
        in_specs=[
            pl.BlockSpec((tm, D_MODEL), lambda i: (i, 0)),
            pl.BlockSpec((1, D_MODEL), lambda i: (0, 0)),
            pl.BlockSpec((D_MODEL, 2 * PEER_DK * PEER_HEADS), lambda i: (0, 0)),
            pl.BlockSpec((2, PEER_HEADS, N_KEYS, PEER_DK), lambda i: (0, 0, 0, 0)),
        ],
        out_specs=[
            pl.BlockSpec((tm, D_MODEL), lambda i: (i, 0)),
            pl.BlockSpec((tm, PEER_PAIRS), lambda i: (i, 0)),
            pl.BlockSpec((tm, PEER_PAIRS), lambda i: (i, 0)),
            pl.BlockSpec((tm, PEER_PAIRS), lambda i: (i, 0)),
        ],
        out_shape=[
            jax.ShapeDtypeStruct((n, D_MODEL), F32),
            jax.ShapeDtypeStruct((n, PEER_PAIRS), I32),
            jax.ShapeDtypeStruct((n, PEER_PAIRS), BF16),
            jax.ShapeDtypeStruct((n, PEER_PAIRS), F32),
        ],
        scratch_shapes=[
            pltpu.VMEM((tm, 2 * PEER_DK * PEER_HEADS), F32),
            pltpu.VMEM((PEER_PAIRS, tm), I32),
            pltpu.VMEM((PEER_PAIRS, tm), F32),
        ],
        compiler_params=_cparams(("parallel",)),
        name="peer_route",
    )(x2, g, wq, keys)


HALF_EXPERTS = N_EXPERTS // 2
GROUP = 2 * SUBLANES
RCOLS = PEER_PAIRS * GROUP
TOK_UNROLL = 4


def _load_table_once(tab_hbm, tab_ref, sem):
    @pl.when(pl.program_id(0) == 0)
    def _():
        cp = pltpu.make_async_copy(tab_hbm, tab_ref, sem)
        cp.start()
        cp.wait()


def _gather_rows(tab_ref, idx_ref, t):
    tiles = [tab_ref[pl.ds(pl.multiple_of(idx_ref[t, k], SUBLANES), SUBLANES), :]
             for k in range(PEER_PAIRS)]
    return pltpu.bitcast(jnp.concatenate(tiles, axis=0), BF16)


def _selector_offsets():
    s = lax.broadcasted_iota(I32, (SUBLANES, RCOLS), 0)
    c = lax.broadcasted_iota(I32, (SUBLANES, RCOLS), 1)
    return ((c & (GROUP - 1)) - 2 * s).astype(F32)


def _peer_u_kernel(idx_ref, tab_hbm, h_ref, gate_ref, hi_ref, e_ref, g_ref, w_ref,
                   tab_ref, sem, hrep_ref, res_ref, *, tb):
    _load_table_once(tab_hbm, tab_ref, sem)
    hrep_ref[...] = jnp.dot(hi_ref[...], e_ref[...], preferred_element_type=F32)
    joff = _selector_offsets()

    def toks(i, carry):
        for u in range(TOK_UNROLL):
            t = i * TOK_UNROLL + u
            r = _gather_rows(tab_ref, idx_ref, t)
            res = _qk(h_ref[t].astype(BF16), r)
            res_ref[t] = jnp.where(joff == hrep_ref[pl.ds(t, 1), :], res, 0.0)
        return carry

    lax.fori_loop(0, tb // TOK_UNROLL, toks, 0)
    res = res_ref[...].reshape(tb * SUBLANES, RCOLS)
    res_hi = res.astype(BF16)
    res_lo = (res - res_hi.astype(F32)).astype(BF16)
    part = (jnp.dot(res_hi, g_ref[...], preferred_element_type=F32)
            + jnp.dot(res_lo, g_ref[...], preferred_element_type=F32))
    act = jnp.sum(part.reshape(tb, SUBLANES, PEER_PAIRS), axis=1)
    gelu = 0.5 * act * (1.0 + lax.erf(act * (2.0 ** -0.5)))
    w_ref[...] = gate_ref[...] * gelu


def _peer_u(idx8, tab, h3, gate, hi, e_mat, g_mat, tb):
    n = idx8.shape[0]
    return pl.pallas_call(
        functools.partial(_peer_u_kernel, tb=tb),
        grid=(n // tb,),
        in_specs=[
            pl.BlockSpec((tb, PEER_PAIRS), lambda i: (i, 0), memory_space=pltpu.SMEM),
            pl.BlockSpec(memory_space=pl.ANY),
            pl.BlockSpec((tb, SUBLANES, LANES), lambda i: (i, 0, 0)),
            pl.BlockSpec((tb, PEER_PAIRS), lambda i: (i, 0)),
            pl.BlockSpec((tb, PEER_PAIRS), lambda i: (i, 0)),
            pl.BlockSpec((PEER_PAIRS, RCOLS), lambda i: (0, 0)),
            pl.BlockSpec((RCOLS, PEER_PAIRS), lambda i: (0, 0)),
        ],
        out_specs=pl.BlockSpec((tb, PEER_PAIRS), lambda i: (i, 0)),
        out_shape=jax.ShapeDtypeStruct((n, PEER_PAIRS), F32),
        scratch_shapes=[
            pltpu.VMEM((HALF_EXPERTS * SUBLANES, LANES), jnp.uint32),
            pltpu.SemaphoreType.DMA(()),
            pltpu.VMEM((tb, RCOLS), F32),
            pltpu.VMEM((tb, SUBLANES, RCOLS), F32),
        ],
        compiler_params=_cparams(("arbitrary",)),
        name="peer_u",
    )(idx8, tab, h3, gate, hi, e_mat, g_mat)


def _peer_v_kernel(idx_ref, tab_hbm, w_ref, hi_ref, e_ref, o_ref, tab_ref, sem,
                   hrep_ref, wrep_ref, *, tb):
    _load_table_once(tab_hbm, tab_ref, sem)
    hrep_ref[...] = jnp.dot(hi_ref[...], e_ref[...], preferred_element_type=F32)
    wrep_ref[...] = jnp.dot(w_ref[...].astype(BF16), e_ref[...], preferred_element_type=F32)
    joff = _selector_offsets()

    def toks(i, carry):
        for u in range(TOK_UNROLL):
            t = i * TOK_UNROLL + u
            r = _gather_rows(tab_ref, idx_ref, t)
            sel = jnp.where(joff == hrep_ref[pl.ds(t, 1), :], wrep_ref[pl.ds(t, 1), :], 0.0)
            o_ref[t] = jnp.dot(sel.astype(BF16), r, preferred_element_type=F32)
        return carry

    lax.fori_loop(0, tb // TOK_UNROLL, toks, 0)


def _peer_v(idx8, tab, w, hi, e_mat, tb):
    n = idx8.shape[0]
    return pl.pallas_call(
        functools.partial(_peer_v_kernel, tb=tb),
        grid=(n // tb,),
        in_specs=[
            pl.BlockSpec((tb, PEER_PAIRS), lambda i: (i, 0), memory_space=pltpu.SMEM),
            pl.BlockSpec(memory_space=pl.ANY),
            pl.BlockSpec((tb, PEER_PAIRS), lambda i: (i, 0)),
            pl.BlockSpec((tb, PEER_PAIRS), lambda i: (i, 0)),
            pl.BlockSpec((PEER_PAIRS, RCOLS), lambda i: (0, 0)),
        ],
        out_specs=pl.BlockSpec((tb, SUBLANES, LANES), lambda i: (i, 0, 0)),
        out_shape=jax.ShapeDtypeStruct((n, SUBLANES, LANES), F32),
        scratch_shapes=[
            pltpu.VMEM((HALF_EXPERTS * SUBLANES, LANES), jnp.uint32),
            pltpu.SemaphoreType.DMA(()),
            pltpu.VMEM((tb, RCOLS), F32),
            pltpu.VMEM((tb, RCOLS), F32),
        ],
        compiler_params=_cparams(("arbitrary",)),
        name="peer_v",
    )(idx8, tab, w, hi, e_mat)


def _ple_kernel(x_ref, peer_ref, p_ref, g_ref, wg_ref, wp_ref, o_ref):
    x = x_ref[...] + peer_ref[...]
    gate = jax.nn.sigmoid(jnp.dot(_rms(x, g_ref[...]).astype(BF16), wg_ref[...],
                                  preferred_element_type=F32))
    emb = jnp.dot(p_ref[...].astype(BF16), wp_ref[...], preferred_element_type=F32)
    o_ref[...] = x + gate * emb


def _ple(x2, peer2, p2, g, wg, wp, tm):
    n = x2.shape[0]
    full = lambda shp: pl.BlockSpec(shp, lambda i: (0, 0))
    return pl.pallas_call(
        _ple_kernel,
        grid=(n // tm,),
        in_specs=[
            pl.BlockSpec((tm, D_MODEL), lambda i: (i, 0)),
            pl.BlockSpec((tm, D_MODEL), lambda i: (i, 0)),
            pl.BlockSpec((tm, PLE_DIM), lambda i: (i, 0)),
            full((1, D_MODEL)), full((D_MODEL, D_MODEL)), full((PLE_DIM, D_MODEL)),
        ],
        out_specs=pl.BlockSpec((tm, D_MODEL), lambda i: (i, 0)),
        out_shape=jax.ShapeDtypeStruct((n, D_MODEL), F32),
        compiler_params=_cparams(("parallel",)),
        name="ple",
    )(x2, peer2, p2, g, wg, wp)


def _relayout_w_in(w_in):
    o = np.cumsum([0, 512, 512, 512, Q_LORA, KV_LORA, MLA_ROPE, 512, 512, 512, 3 * D_MODEL])
    seg = lambda k: w_in[:, o[k]:o[k + 1]]
    pad = jnp.zeros((D_MODEL, 512 - Q_LORA - KV_LORA - MLA_ROPE), w_in.dtype)
    return jnp.concatenate([seg(9), seg(0), seg(1), seg(2), seg(6), seg(7), seg(8),
                            seg(3), seg(4), seg(5), pad], axis=1).astype(BF16)


def _pad_heads(w, width):
    k = w.shape[0]
    w3 = w.reshape(k, MLA_HEADS, width)
    return jnp.pad(w3, ((0, 0), (0, 0), (0, LANES - width))).reshape(k, MLA_HEADS * LANES)


def _pack_table(tab):
    bits = lax.bitcast_convert_type(tab.astype(BF16), jnp.uint16).astype(jnp.uint32)
    packed = bits[:HALF_EXPERTS] | (bits[HALF_EXPERTS:] << 16)
    return packed.reshape(HALF_EXPERTS * SUBLANES, LANES)


def _group_matrices():
    e = (np.arange(RCOLS)[None, :] // GROUP == np.arange(PEER_PAIRS)[:, None])
    return jnp.asarray(e, BF16), jnp.asarray(e.T, BF16)


def _rope_tables(seq):
    half = MLA_ROPE // 2
    freqs = ROPE_THETA ** (-np.arange(half, dtype=np.float32) / half)
    ang = np.arange(seq, dtype=np.float32)[:, None] * freqs[None, :]
    cos = np.ones((seq, LANES), np.float32)
    sin = np.zeros((seq, LANES), np.float32)
    cos[:, MLA_NOPE:MLA_NOPE + half] = np.cos(ang)
    cos[:, MLA_NOPE + half:MLA_QK] = np.cos(ang)
    sin[:, MLA_NOPE:MLA_NOPE + half] = -np.sin(ang)
    sin[:, MLA_NOPE + half:MLA_QK] = np.sin(ang)
    return jnp.asarray(cos), jnp.asarray(sin)


def _tile(n, pref):
    return pref if n % pref == 0 else n


def _layer(x2, p2, layer_idx, batch, seq, g_mix, w_in, g_diff_q, g_diff_k, lam_q1, lam_k1, lam_q2,
           lam_k2, g_diff_sub, g_cq, w_uq, g_ckv, w_ukv, g_mla_q, g_mla_k, conv_w, w_br_diff,
           w_br_mla, w_br_conv, w_out, g_ffn, w_query, sub_keys, u_experts, v_experts, g_ple,
           w_ple_gate, w_ple_proj):
    n = x2.shape[0]
    lambda_init = 0.8 - 0.6 * math.exp(-0.3 * layer_idx)
    tm = _tile(seq, 256)
    tq = _tile(seq, 512)
    row = lambda v: v.reshape(1, -1)

    z2 = _inproj(x2, row(g_mix), _relayout_w_in(w_in), tm)
    z3 = z2.reshape(batch, seq, Z_W)

    slopes = jnp.asarray(2.0 ** (-8.0 * np.arange(1, DIFF_HEADS + 1, dtype=np.float32) / DIFF_HEADS))
    gq2 = jnp.tile(g_diff_q, 2).reshape(1, LANES)
    gk2 = jnp.tile(g_diff_k, 2).reshape(1, LANES)
    lam4 = jnp.stack([lam_q1, lam_k1, lam_q2, lam_k2])
    yd = _diff_attention(z3, slopes, gq2, gk2, lam4, row(g_diff_sub), tq, lambda_init)

    w_ukv3 = w_ukv.reshape(KV_LORA, MLA_HEADS, MLA_NOPE + MLA_V)
    wuk_p = _pad_heads(w_ukv3[:, :, :MLA_NOPE].reshape(KV_LORA, -1), MLA_NOPE).astype(BF16)
    wuv_p = w_ukv3[:, :, MLA_NOPE:].reshape(KV_LORA, -1).astype(BF16)
    wuq_p = _pad_heads(w_uq, MLA_QK).astype(BF16)
    gq_p = jnp.pad(g_mla_q, (0, LANES - MLA_QK)).reshape(1, LANES)
    gk_p = jnp.pad(g_mla_k, (0, LANES - MLA_QK)).reshape(1, LANES)
    cos_t, sin_t = _rope_tables(seq)
    qm, km, vm = _mla_proj(z2, row(g_cq), wuq_p, row(g_ckv), wuk_p, wuv_p, gq_p, gk_p,
                           cos_t, sin_t, tm, seq)
    ym = _mla_attention(qm.reshape(batch, seq, -1), km.reshape(batch, seq, -1),
                        vm.reshape(batch, seq, -1), tq)

    x2 = _merge(x2, yd.reshape(n, DIFF_W), ym.reshape(n, -1), z2, conv_w,
                w_br_diff.astype(BF16), w_br_mla.astype(BF16), w_br_conv.astype(BF16),
                w_out.astype(BF16), tm, seq)

    h2, idx8, hi, gate = _route(x2, row(g_ffn), w_query.astype(BF16), sub_keys.astype(BF16), tm)
    tb = _tile(n, 64)
    e_mat, g_mat = _group_matrices()
    w = _peer_u(idx8, _pack_table(u_experts), h2.reshape(n, SUBLANES, LANES), gate, hi,
                e_mat, g_mat, tb)
    peer = _peer_v(idx8, _pack_table(v_experts), w, hi, e_mat, tb)

    return _ple(x2, peer.reshape(n, D_MODEL), p2, row(g_ple), w_ple_gate.astype(BF16),
                w_ple_proj.astype(BF16), tm)


def kernel(x, p, g_mix, w_in, g_diff_q, g_diff_k, lam_q1, lam_k1, lam_q2, lam_k2, g_diff_sub, g_cq,
           w_uq, g_ckv, w_ukv, g_mla_q, g_mla_k, conv_w, w_br_diff, w_br_mla, w_br_conv, w_out,
           g_ffn, w_query, sub_keys, u_experts, v_experts, g_ple, w_ple_gate, w_ple_proj):
    batch, seq, _ = x.shape
    n = batch * seq
    x2 = x.reshape(n, D_MODEL)
    per_layer = (g_mix, w_in, g_diff_q, g_diff_k, lam_q1, lam_k1, lam_q2, lam_k2, g_diff_sub, g_cq,
                 w_uq, g_ckv, w_ukv, g_mla_q, g_mla_k, conv_w, w_br_diff, w_br_mla, w_br_conv,
                 w_out, g_ffn, w_query, sub_keys, u_experts, v_experts, g_ple, w_ple_gate,
                 w_ple_proj)
    for i in range(p.shape[0]):
        x2 = _layer(x2, p[i].reshape(n, PLE_DIM), i, batch, seq, *(a[i] for a in per_layer))
    return x2.reshape(batch, seq, D_MODEL)
```

```python
import functools
import math

import jax
import jax.numpy as jnp
import numpy as np
from jax import lax
from jax.experimental import pallas as pl
from jax.experimental.pallas import tpu as pltpu

F32 = jnp.float32
BF16 = jnp.bfloat16
I32 = jnp.int32

D_MODEL = 1024
DIFF_HEADS = 4
DIFF_HD = 64
DIFF_W = 512
MLA_HEADS = 8
MLA_NOPE = 64
MLA_ROPE = 32
MLA_V = 64
MLA_QK = 96
Q_LORA = 256
KV_LORA = 128
ROPE_THETA = 10000.0
CONV_W = 512
CONV_K = 3
PEER_HEADS = 8
N_KEYS = 128
N_EXPERTS = N_KEYS * N_KEYS
PEER_DK = 128
K_SUB = 16
PEER_TOPK = 16
PEER_PAIRS = PEER_HEADS * PEER_TOPK
PLE_DIM = 256
NORM_EPS = 1e-6
NEG_INF = -1e30

LANES = 128
SUBLANES = 8
VMEM_LIMIT = 56 * 1024 * 1024

Z_GATES = 0
Z_DQ = 3072
Z_DK = 3584
Z_DV = 4096
Z_CB = 4608
Z_CC = 5120
Z_CX = 5632
Z_MLA = 6144
Z_W = 6656


def _cparams(sem, vmem=VMEM_LIMIT):
    return pltpu.CompilerParams(dimension_semantics=sem, vmem_limit_bytes=vmem)


def _rms(x, g):
    return x * lax.rsqrt(jnp.mean(x * x, axis=-1, keepdims=True) + NORM_EPS) * g


def _inproj_kernel(x_ref, g_ref, w_ref, z_ref):
    h = _rms(x_ref[...], g_ref[...]).astype(BF16)
    z_ref[...] = jnp.dot(h, w_ref[...], preferred_element_type=F32)


def _inproj(x2, g, w_in_r, tm):
    n = x2.shape[0]
    return pl.pallas_call(
        _inproj_kernel,
        grid=(n // tm,),
        in_specs=[
            pl.BlockSpec((tm, D_MODEL), lambda i: (i, 0)),
            pl.BlockSpec((1, D_MODEL), lambda i: (0, 0)),
            pl.BlockSpec((D_MODEL, Z_W), lambda i: (0, 0)),
        ],
        out_specs=pl.BlockSpec((tm, Z_W), lambda i: (i, 0)),
        out_shape=jax.ShapeDtypeStruct((n, Z_W), F32),
        compiler_params=_cparams(("parallel",)),
        name="inproj",
    )(x2, g, w_in_r)


def _qk(q, kt):
    return lax.dot_general(q, kt, (((1,), (1,)), ((), ())), preferred_element_type=F32)


def _flash_update(s, m_ref, l_ref, acc_ref, vt, diag):
    tq, tk = s.shape
    if diag:
        row = lax.broadcasted_iota(I32, s.shape, 0)
        col = lax.broadcasted_iota(I32, s.shape, 1)
        s = jnp.where(row >= col, s, NEG_INF)
    m_old = m_ref[...]
    m_new = jnp.maximum(m_old, jnp.max(s, axis=-1, keepdims=True))
    p = jnp.exp(s - jnp.tile(m_new, (1, tk // LANES)))
    alpha = jnp.exp(m_old - m_new)
    l_ref[...] = alpha * l_ref[...] + jnp.sum(p, axis=-1, keepdims=True)
    m_ref[...] = m_new
    acc_ref[...] = alpha * acc_ref[...] + jnp.dot(p.astype(BF16), vt, preferred_element_type=F32)


def _flash_init(m_ref, l_ref, acc_ref):
    m_ref[...] = jnp.full(m_ref.shape, NEG_INF, F32)
    l_ref[...] = jnp.zeros(l_ref.shape, F32)
    acc_ref[...] = jnp.zeros(acc_ref.shape, F32)


def _flash_scratch(tq):
    return [pltpu.VMEM((tq, LANES), F32)] * 3


ALIBI_BITS = 6
ALIBI_SPLIT = 1 << ALIBI_BITS


def _diff_attn_kernel(slope_ref, q_ref, k_ref, v_ref, gq_ref, gk_ref, lam_ref, gsub_ref, o_ref,
                      k1_ref, k2_ref, m1_ref, l1_ref, a1_ref, m2_ref, l2_ref, a2_ref,
                      *, tq, seq, lambda_init):
    h = pl.program_id(1)
    qi = pl.program_id(2)
    lane = lax.broadcasted_iota(I32, (1, LANES), 1)
    first = lane < DIFF_HD
    slope = slope_ref[h]

    def halfnorm(x, g):
        x2 = x * x
        s_all = jnp.sum(x2, axis=-1, keepdims=True)
        s_lo = jnp.sum(jnp.where(first, x2, 0.0), axis=-1, keepdims=True)
        ms = jnp.where(first, s_lo, s_all - s_lo) * (1.0 / DIFF_HD)
        return x * lax.rsqrt(ms + NORM_EPS) * g

    def alibi_cols(pos0, base, query):
        pos = lax.broadcasted_iota(I32, (tq, 1), 0) + pos0
        hi = lax.shift_right_logical(pos, ALIBI_BITS).astype(F32) * (slope * ALIBI_SPLIT)
        lo = (pos & (ALIBI_SPLIT - 1)).astype(F32) * slope
        if query:
            c0, c1, c2, c3 = -hi, -lo, 1.0, 1.0
        else:
            c0, c1, c2, c3 = 1.0, 1.0, hi, lo
        return jnp.where(lane == base, c0, jnp.where(lane == base + 1, c1, jnp.where(
            lane == base + 2, c2, jnp.where(lane == base + 3, c3, 0.0))))

    @pl.when(qi == 0)
    def _():
        def body(c, carry):
            r0 = pl.multiple_of(c * tq, tq)
            kn = halfnorm(k_ref[pl.ds(r0, tq), :], gk_ref[...])
            k1_ref[pl.ds(r0, tq), :] = jnp.where(first, kn, alibi_cols(r0, DIFF_HD, False)).astype(BF16)
            k2_ref[pl.ds(r0, tq), :] = jnp.where(first, alibi_cols(r0, 0, False), kn).astype(BF16)
            return carry
        lax.fori_loop(0, seq // tq, body, 0)

    q = halfnorm(q_ref[...], gq_ref[...]) * (DIFF_HD ** -0.5)
    q1 = jnp.where(first, q, alibi_cols(qi * tq, DIFF_HD, True)).astype(BF16)
    q2 = jnp.where(first, alibi_cols(qi * tq, 0, True), q).astype(BF16)
    _flash_init(m1_ref, l1_ref, a1_ref)
    _flash_init(m2_ref, l2_ref, a2_ref)

    def step(j, diag):
        rows = pl.ds(pl.multiple_of(j * tq, tq), tq)
        vt = v_ref[rows, :].astype(BF16)
        _flash_update(_qk(q1, k1_ref[rows, :]), m1_ref, l1_ref, a1_ref, vt, diag)
        _flash_update(_qk(q2, k2_ref[rows, :]), m2_ref, l2_ref, a2_ref, vt, diag)

    def off_diag(j, carry):
        step(j, False)
        return carry

    lax.fori_loop(0, qi, off_diag, 0)
    step(qi, True)

    lam = lam_ref[...]
    e1 = jnp.exp(jnp.sum(lam[0:1] * lam[1:2], axis=-1, keepdims=True))
    e2 = jnp.exp(jnp.sum(lam[2:3] * lam[3:4], axis=-1, keepdims=True))
    lam_full = e1 - e2 + lambda_init
    o = a1_ref[...] / l1_ref[...] - lam_full * (a2_ref[...] / l2_ref[...])
    o_ref[...] = _rms(o, gsub_ref[...]) * (1.0 - lambda_init)


def _diff_attention(z3, slopes, gq2, gk2, lam4, gsub, tq, lambda_init):
    b, s, _ = z3.shape
    kern = functools.partial(_diff_attn_kernel, tq=tq, seq=s, lambda_init=lambda_init)
    return pl.pallas_call(
        kern,
        grid=(b, DIFF_HEADS, s // tq),
        in_specs=[
            pl.BlockSpec(memory_space=pltpu.SMEM),
            pl.BlockSpec((None, tq, LANES), lambda bi, h, qi: (bi, qi, Z_DQ // LANES + h)),
            pl.BlockSpec((None, s, LANES), lambda bi, h, qi: (bi, 0, Z_DK // LANES + h)),
            pl.BlockSpec((None, s, LANES), lambda bi, h, qi: (bi, 0, Z_DV // LANES + h)),
            pl.BlockSpec((1, LANES), lambda bi, h, qi: (0, 0)),
            pl.BlockSpec((1, LANES), lambda bi, h, qi: (0, 0)),
            pl.BlockSpec((4, DIFF_HD), lambda bi, h, qi: (0, 0)),
            pl.BlockSpec((1, LANES), lambda bi, h, qi: (0, 0)),
        ],
        out_specs=pl.BlockSpec((None, tq, LANES), lambda bi, h, qi: (bi, qi, h)),
        out_shape=jax.ShapeDtypeStruct((b, s, DIFF_W), F32),
        scratch_shapes=[pltpu.VMEM((s, LANES), BF16), pltpu.VMEM((s, LANES), BF16)]
        + _flash_scratch(tq) + _flash_scratch(tq),
        compiler_params=_cparams(("parallel", "parallel", "arbitrary")),
        name="diff_attn",
    )(slopes, z3, z3, z3, gq2, gk2, lam4, gsub)


def _mla_proj_kernel(z_ref, gcq_ref, wuq_ref, gckv_ref, wuk_ref, wuv_ref, gq_ref, gk_ref,
                     cos_ref, sin_ref, qm_ref, km_ref, vm_ref):
    z = z_ref[...]
    cq = z[:, :Q_LORA]
    ckv = z[:, Q_LORA:Q_LORA + KV_LORA]
    krb = z[:, Q_LORA + KV_LORA:]
    cqn = _rms(cq, gcq_ref[...]).astype(BF16)
    ckvn = _rms(ckv, gckv_ref[...]).astype(BF16)
    q = jnp.dot(cqn, wuq_ref[...], preferred_element_type=F32)
    kn = jnp.dot(ckvn, wuk_ref[...], preferred_element_type=F32)
    v = jnp.dot(ckvn, wuv_ref[...], preferred_element_type=F32)
    vm_ref[...] = v.astype(BF16)

    lane = lax.broadcasted_iota(I32, (1, LANES), 1)
    in_rope = (lane >= MLA_NOPE) & (lane < MLA_QK)
    in_x1 = (lane >= MLA_NOPE) & (lane < MLA_NOPE + MLA_ROPE // 2)
    kr = jnp.where(in_rope, pltpu.roll(krb, MLA_NOPE, 1), 0.0)
    cos = cos_ref[...]
    sin = sin_ref[...]
    gq = gq_ref[...]
    gk = gk_ref[...]
    half = MLA_ROPE // 2

    def norm_rope(xh, g):
        ms = jnp.sum(xh * xh, axis=-1, keepdims=True) * (1.0 / MLA_QK)
        xn = xh * lax.rsqrt(ms + NORM_EPS) * g
        sw = jnp.where(in_x1, pltpu.roll(xn, LANES - half, 1), pltpu.roll(xn, half, 1))
        return xn * cos + sw * sin

    for hh in range(MLA_HEADS):
        sl = slice(hh * LANES, (hh + 1) * LANES)
        qm_ref[:, sl] = (norm_rope(q[:, sl], gq) * (MLA_QK ** -0.5)).astype(BF16)
        km_ref[:, sl] = norm_rope(kn[:, sl] + kr, gk).astype(BF16)


def _mla_proj(z2, gcq, wuq_p, gckv, wuk_p, wuv_p, gq_p, gk_p, cos_t, sin_t, tm, seq):
    n = z2.shape[0]
    nsb = seq // tm
    full = lambda shp: pl.BlockSpec(shp, lambda i: (0, 0))
    return pl.pallas_call(
        _mla_proj_kernel,
        grid=(n // tm,),
        in_specs=[
            pl.BlockSpec((tm, 512), lambda i: (i, Z_MLA // 512)),
            full((1, Q_LORA)), full((Q_LORA, MLA_HEADS * LANES)),
            full((1, KV_LORA)), full((KV_LORA, MLA_HEADS * LANES)), full((KV_LORA, MLA_HEADS * MLA_V)),
            full((1, LANES)), full((1, LANES)),
            pl.BlockSpec((tm, LANES), lambda i: (i % nsb, 0)),
            pl.BlockSpec((tm, LANES), lambda i: (i % nsb, 0)),
        ],
        out_specs=[
            pl.BlockSpec((tm, MLA_HEADS * LANES), lambda i: (i, 0)),
            pl.BlockSpec((tm, MLA_HEADS * LANES), lambda i: (i, 0)),
            pl.BlockSpec((tm, MLA_HEADS * MLA_V), lambda i: (i, 0)),
        ],
        out_shape=[
            jax.ShapeDtypeStruct((n, MLA_HEADS * LANES), BF16),
            jax.ShapeDtypeStruct((n, MLA_HEADS * LANES), BF16),
            jax.ShapeDtypeStruct((n, MLA_HEADS * MLA_V), BF16),
        ],
        compiler_params=_cparams(("parallel",)),
        name="mla_proj",
    )(z2, gcq, wuq_p, gckv, wuk_p, wuv_p, gq_p, gk_p, cos_t, sin_t)


def _mla_attn_kernel(q_ref, k_ref, v_ref, o_ref, ma_ref, la_ref, aa_ref, mb_ref, lb_ref, ab_ref,
                     *, tq):
    qi = pl.program_id(2)
    _flash_init(ma_ref, la_ref, aa_ref)
    _flash_init(mb_ref, lb_ref, ab_ref)

    def step(j, diag):
        rows = pl.ds(pl.multiple_of(j * tq, tq), tq)
        vt = v_ref[rows, :]
        _flash_update(_qk(q_ref[:, :LANES], k_ref[rows, :LANES]), ma_ref, la_ref, aa_ref, vt, diag)
        _flash_update(_qk(q_ref[:, LANES:], k_ref[rows, LANES:]), mb_ref, lb_ref, ab_ref, vt, diag)

    def off_diag(j, carry):
        step(j, False)
        return carry

    lax.fori_loop(0, qi, off_diag, 0)
    step(qi, True)
    lane = lax.broadcasted_iota(I32, (1, LANES), 1)
    o_ref[...] = jnp.where(lane < MLA_V, aa_ref[...] / la_ref[...], ab_ref[...] / lb_ref[...])


def _mla_attention(qm3, km3, vm3, tq):
    b, s, _ = qm3.shape
    return pl.pallas_call(
        functools.partial(_mla_attn_kernel, tq=tq),
        grid=(b, MLA_HEADS // 2, s // tq),
        in_specs=[
            pl.BlockSpec((None, tq, 2 * LANES), lambda bi, hp, qi: (bi, qi, hp)),
            pl.BlockSpec((None, s, 2 * LANES), lambda bi, hp, qi: (bi, 0, hp)),
            pl.BlockSpec((None, s, LANES), lambda bi, hp, qi: (bi, 0, hp)),
        ],
        out_specs=pl.BlockSpec((None, tq, LANES), lambda bi, hp, qi: (bi, qi, hp)),
        out_shape=jax.ShapeDtypeStruct((b, s, MLA_HEADS * MLA_V), F32),
        scratch_shapes=_flash_scratch(tq) + _flash_scratch(tq),
        compiler_params=_cparams(("parallel", "parallel", "arbitrary")),
        name="mla_attn",
    )(qm3, km3, vm3)


def _merge_kernel(x_ref, yd_ref, ym_ref, cb_ref, cc_ref, cx_ref, cch_ref, cxh_ref,
                  g0_ref, g1_ref, g2_ref, cw_ref, wd_ref, wm_ref, wc_ref, wo_ref,
                  o_ref, ucat_ref, *, tm, seq):
    i = pl.program_id(0)
    at_seq_start = (i * tm) % seq == 0
    halo = cch_ref[...] * cxh_ref[...]
    ucat_ref[0:SUBLANES, :] = jnp.where(at_seq_start, 0.0, halo)
    ucat_ref[SUBLANES:, :] = cc_ref[...] * cx_ref[...]
    cw = cw_ref[...]
    conv = (cw[0:1] * ucat_ref[SUBLANES - 2:SUBLANES - 2 + tm, :]
            + cw[1:2] * ucat_ref[SUBLANES - 1:SUBLANES - 1 + tm, :]
            + cw[2:3] * ucat_ref[SUBLANES:, :])
    yc = cb_ref[...] * conv

    def br(y, w_ref, g_ref):
        return jax.nn.sigmoid(g_ref[...]) * jnp.dot(y.astype(BF16), w_ref[...],
                                                    preferred_element_type=F32)

    mixed = br(yd_ref[...], wd_ref, g0_ref) + br(ym_ref[...], wm_ref, g1_ref) + br(yc, wc_ref, g2_ref)
    o_ref[...] = x_ref[...] + jnp.dot(mixed.astype(BF16), wo_ref[...], preferred_element_type=F32)


def _merge(x2, yd2, ym2, z2, conv_w, wd, wm, wc, wo, tm, seq):
    n = x2.shape[0]
    hb = tm // SUBLANES
    full = lambda shp: pl.BlockSpec(shp, lambda i: (0, 0))
    zc = lambda off: pl.BlockSpec((tm, 512), lambda i: (i, off // 512))
    zh = lambda off: pl.BlockSpec((SUBLANES, 512), lambda i: (jnp.maximum(i * hb - 1, 0), off // 512))
    zg = lambda k: pl.BlockSpec((tm, D_MODEL), lambda i: (i, k))
    return pl.pallas_call(
        functools.partial(_merge_kernel, tm=tm, seq=seq),
        grid=(n // tm,),
        in_specs=[
            pl.BlockSpec((tm, D_MODEL), lambda i: (i, 0)),
            pl.BlockSpec((tm, DIFF_W), lambda i: (i, 0)),
            pl.BlockSpec((tm, 512), lambda i: (i, 0)),
            zc(Z_CB), zc(Z_CC), zc(Z_CX), zh(Z_CC), zh(Z_CX),
            zg(0), zg(1), zg(2),
            full((CONV_K, CONV_W)),
            full((DIFF_W, D_MODEL)), full((512, D_MODEL)), full((CONV_W, D_MODEL)),
            full((D_MODEL, D_MODEL)),
        ],
        out_specs=pl.BlockSpec((tm, D_MODEL), lambda i: (i, 0)),
        out_shape=jax.ShapeDtypeStruct((n, D_MODEL), F32),
        scratch_shapes=[pltpu.VMEM((tm + SUBLANES, CONV_W), F32)],
        compiler_params=_cparams(("parallel",)),
        name="merge",
    )(x2, yd2, ym2, z2, z2, z2, z2, z2, z2, z2, z2, conv_w, wd, wm, wc, wo)


def _topk_rows(s, k):
    rows, t = s.shape
    riota = lax.broadcasted_iota(I32, (rows, t), 0)
    kiota = lax.broadcasted_iota(I32, (k, t), 0)
    vals = jnp.zeros((k, t), F32)
    idxs = jnp.zeros((k, t), I32)
    for i in range(k):
        m = jnp.max(s, axis=0, keepdims=True)
        am = jnp.min(jnp.where(s == m, riota, rows), axis=0, keepdims=True)
        vals = jnp.where(kiota == i, m, vals)
        idxs = jnp.where(kiota == i, am, idxs)
        s = jnp.where(riota == am, -jnp.inf, s)
    return vals, idxs


def _route_kernel(x_ref, g_ref, wq_ref, keys_ref, h_ref, idx_ref, hi_ref, gate_ref,
                  q_ref, expt_ref, gatet_ref, *, tm):
    h2 = _rms(x_ref[...], g_ref[...])
    h_ref[...] = h2
    q_ref[...] = jnp.dot(h2.astype(BF16), wq_ref[...], preferred_element_type=F32)

    def head(hh, carry):
        tops = []
        for p in range(2):
            c0 = pl.multiple_of(hh * (2 * PEER_DK) + p * PEER_DK, PEER_DK)
            qhp = q_ref[:, pl.ds(c0, PEER_DK)].astype(BF16)
            keys = keys_ref[p, hh]
            st = _qk(keys, qhp)
            tops.append(_topk_rows(st, K_SUB))
        (s0, i0), (s1, i1) = tops
        nb = lambda a: K_SUB if a == 0 else SUBLANES
        cand = jnp.concatenate([s0[a:a + 1] + s1[:nb(a)] for a in range(K_SUB)], axis=0)
        cexp = jnp.concatenate([i0[a:a + 1] * N_KEYS + i1[:nb(a)] for a in range(K_SUB)], axis=0)
        nc = K_SUB + (K_SUB - 1) * SUBLANES
        riota = lax.broadcasted_iota(I32, (nc, tm), 0)
        kiota = lax.broadcasted_iota(I32, (PEER_TOPK, tm), 0)
        best = jnp.zeros((PEER_TOPK, tm), F32)
        bexp = jnp.zeros((PEER_TOPK, tm), I32)
        for i in range(PEER_TOPK):
            m = jnp.max(cand, axis=0, keepdims=True)
            am = jnp.min(jnp.where(cand == m, riota, nc), axis=0, keepdims=True)
            hit = riota == am
            e = jnp.max(jnp.where(hit, cexp, -1), axis=0, keepdims=True)
            best = jnp.where(kiota == i, m, best)
            bexp = jnp.where(kiota == i, e, bexp)
            cand = jnp.where(hit, -jnp.inf, cand)
        ex = jnp.exp(best - best[0:1])
        gate = ex / jnp.sum(ex, axis=0, keepdims=True)
        r0 = pl.multiple_of(hh * PEER_TOPK, PEER_TOPK)
        expt_ref[pl.ds(r0, PEER_TOPK), :] = bexp
        gatet_ref[pl.ds(r0, PEER_TOPK), :] = gate
        return carry

    lax.fori_loop(0, PEER_HEADS, head, 0)
    expert = expt_ref[...].T
    idx_ref[...] = (expert & (N_EXPERTS // 2 - 1)) * SUBLANES
    hi_ref[...] = (expert >> 13).astype(F32).astype(BF16)
    gate_ref[...] = gatet_ref[...].T


def _route(x2, g, wq, keys, tm):
    n = x2.shape[0]
    return pl.pallas_call(
        functools.partial(_route_kernel, tm=tm),
        grid=(n // tm,),
        in_specs=[
            pl.BlockSpec((tm, D_MODEL), lambda i: (i, 0)),
            pl.BlockSpec((1, D_MODEL), lambda i: (0, 0)),
            pl.BlockSpec((D_MODEL, 2 * PEER_DK * PEER_HEADS), lambda i: (0, 0)),
            pl.BlockSpec((2, PEER_HEADS, N_KEYS, PEER_DK), lambda i: (0, 0, 0, 0)),
        ],
        out_specs=[
            pl.BlockSpec((tm, D_MODEL), lambda i: (i, 0)),
            pl.BlockSpec((tm, PEER_PAIRS), lambda i: (i, 0)),
            pl.BlockSpec((tm, PEER_PAIRS), lambda i: (i, 0)),
            pl.BlockSpec((tm, PEER_PAIRS), lambda i: (i, 0)),
        ],
        out_shape=[
            jax.ShapeDtypeStruct((n, D_MODEL), F32),
            jax.ShapeDtypeStruct((n, PEER_PAIRS), I32),
            jax.ShapeDtypeStruct((n, PEER_PAIRS), BF16),
            jax.ShapeDtypeStruct((n, PEER_PAIRS), F32),
        ],
        scratch_shapes=[
            pltpu.VMEM((tm, 2 * PEER_DK * PEER_HEADS), F32),
            pltpu.VMEM((PEER_PAIRS, tm), I32),
            pltpu.VMEM((PEER_PAIRS, tm), F32),
        ],
        compiler_params=_cparams(("parallel",)),
        name="peer_route",
    )(x2, g, wq, keys)


HALF_EXPERTS = N_EXPERTS // 2
GROUP = 2 * SUBLANES
RCOLS = PEER_PAIRS * GROUP
PEER_TB = 16


def _table_spec():
    return pl.BlockSpec((HALF_EXPERTS * SUBLANES, LANES), lambda i: (0, 0),
                        pipeline_mode=pl.Buffered(1))


def _gather_rows(tab_ref, idx_ref, t):
    idx_t = idx_ref.at[t]
    tiles = [tab_ref[pl.ds(pl.multiple_of(idx_t[k], SUBLANES), SUBLANES), :]
             for k in range(PEER_PAIRS)]
    return pltpu.bitcast(jnp.concatenate(tiles, axis=0), BF16)


def _selector_offsets():
    s = lax.broadcasted_iota(I32, (SUBLANES, RCOLS), 0)
    c = lax.broadcasted_iota(I32, (SUBLANES, RCOLS), 1)
    return ((c & (GROUP - 1)) - 2 * s).astype(F32)


def _peer_u_kernel(idx_ref, tab_ref, h_ref, gate_ref, hi_ref, e_ref, g_ref, w_ref,
                   hrep_ref, res_ref, *, tb):
    hrep_ref[...] = jnp.dot(hi_ref[...], e_ref[...], preferred_element_type=F32)
    joff = _selector_offsets()

    for t in range(tb):
        r = _gather_rows(tab_ref, idx_ref, t)
        res = _qk(h_ref[t].astype(BF16), r)
        res_ref[t] = jnp.where(joff == hrep_ref[pl.ds(t, 1), :], res, 0.0)
    res = res_ref[...].reshape(tb * SUBLANES, RCOLS)
    res_hi = res.astype(BF16)
    res_lo = (res - res_hi.astype(F32)).astype(BF16)
    part = (jnp.dot(res_hi, g_ref[...], preferred_element_type=F32)
            + jnp.dot(res_lo, g_ref[...], preferred_element_type=F32))
    act = jnp.sum(part.reshape(tb, SUBLANES, PEER_PAIRS), axis=1)
    gelu = 0.5 * act * (1.0 + lax.erf(act * (2.0 ** -0.5)))
    w_ref[...] = gate_ref[...] * gelu


def _peer_u(idx8, tab, h3, gate, hi, e_mat, g_mat, tb):
    n = idx8.shape[0]
    return pl.pallas_call(
        functools.partial(_peer_u_kernel, tb=tb),
        grid=(n // tb,),
        in_specs=[
            pl.BlockSpec((tb, PEER_PAIRS), lambda i: (i, 0), memory_space=pltpu.SMEM),
            _table_spec(),
            pl.BlockSpec((tb, SUBLANES, LANES), lambda i: (i, 0, 0)),
            pl.BlockSpec((tb, PEER_PAIRS), lambda i: (i, 0)),
            pl.BlockSpec((tb, PEER_PAIRS), lambda i: (i, 0)),
            pl.BlockSpec((PEER_PAIRS, RCOLS), lambda i: (0, 0)),
            pl.BlockSpec((RCOLS, PEER_PAIRS), lambda i: (0, 0)),
        ],
        out_specs=pl.BlockSpec((tb, PEER_PAIRS), lambda i: (i, 0)),
        out_shape=jax.ShapeDtypeStruct((n, PEER_PAIRS), F32),
        scratch_shapes=[
            pltpu.VMEM((tb, RCOLS), F32),
            pltpu.VMEM((tb, SUBLANES, RCOLS), F32),
        ],
        compiler_params=_cparams(("arbitrary",)),
        name="peer_u",
    )(idx8, tab, h3, gate, hi, e_mat, g_mat)


def _peer_v_kernel(idx_ref, tab_ref, w_ref, hi_ref, e_ref, o_ref, hrep_ref, wrep_ref, *, tb):
    hrep_ref[...] = jnp.dot(hi_ref[...], e_ref[...], preferred_element_type=F32)
    wrep_ref[...] = jnp.dot(w_ref[...].astype(BF16), e_ref[...], preferred_element_type=F32)
    joff = _selector_offsets()

    for t in range(tb):
        r = _gather_rows(tab_ref, idx_ref, t)
        sel = jnp.where(joff == hrep_ref[pl.ds(t, 1), :], wrep_ref[pl.ds(t, 1), :], 0.0)
        o_ref[t] = jnp.dot(sel.astype(BF16), r, preferred_element_type=F32)


def _peer_v(idx8, tab, w, hi, e_mat, tb):
    n = idx8.shape[0]
    return pl.pallas_call(
        functools.partial(_peer_v_kernel, tb=tb),
        grid=(n // tb,),
        in_specs=[
            pl.BlockSpec((tb, PEER_PAIRS), lambda i: (i, 0), memory_space=pltpu.SMEM),
            _table_spec(),
            pl.BlockSpec((tb, PEER_PAIRS), lambda i: (i, 0)),
            pl.BlockSpec((tb, PEER_PAIRS), lambda i: (i, 0)),
            pl.BlockSpec((PEER_PAIRS, RCOLS), lambda i: (0, 0)),
        ],
        out_specs=pl.BlockSpec((tb, SUBLANES, LANES), lambda i: (i, 0, 0)),
        out_shape=jax.ShapeDtypeStruct((n, SUBLANES, LANES), F32),
        scratch_shapes=[
            pltpu.VMEM((tb, RCOLS), F32),
            pltpu.VMEM((tb, RCOLS), F32),
        ],
        compiler_params=_cparams(("arbitrary",)),
        name="peer_v",
    )(idx8, tab, w, hi, e_mat)


def _ple_kernel(x_ref, peer_ref, p_ref, g_ref, wg_ref, wp_ref, o_ref):
    x = x_ref[...] + peer_ref[...]
    gate = jax.nn.sigmoid(jnp.dot(_rms(x, g_ref[...]).astype(BF16), wg_ref[...],
                                  preferred_element_type=F32))
    emb = jnp.dot(p_ref[...].astype(BF16), wp_ref[...], preferred_element_type=F32)
    o_ref[...] = x + gate * emb


def _ple(x2, peer2, p2, g, wg, wp, tm):
    n = x2.shape[0]
    full = lambda shp: pl.BlockSpec(shp, lambda i: (0, 0))
    return pl.pallas_call(
        _ple_kernel,
        grid=(n // tm,),
        in_specs=[
            pl.BlockSpec((tm, D_MODEL), lambda i: (i, 0)),
            pl.BlockSpec((tm, D_MODEL), lambda i: (i, 0)),
            pl.BlockSpec((tm, PLE_DIM), lambda i: (i, 0)),
            full((1, D_MODEL)), full((D_MODEL, D_MODEL)), full((PLE_DIM, D_MODEL)),
        ],
        out_specs=pl.BlockSpec((tm, D_MODEL), lambda i: (i, 0)),
        out_shape=jax.ShapeDtypeStruct((n, D_MODEL), F32),
        compiler_params=_cparams(("parallel",)),
        name="ple",
    )(x2, peer2, p2, g, wg, wp)


def _relayout_w_in(w_in):
    o = np.cumsum([0, 512, 512, 512, Q_LORA, KV_LORA, MLA_ROPE, 512, 512, 512, 3 * D_MODEL])
    seg = lambda k: w_in[:, o[k]:o[k + 1]]
    pad = jnp.zeros((D_MODEL, 512 - Q_LORA - KV_LORA - MLA_ROPE), w_in.dtype)
    return jnp.concatenate([seg(9), seg(0), seg(1), seg(2), seg(6), seg(7), seg(8),
                            seg(3), seg(4), seg(5), pad], axis=1).astype(BF16)


def _pad_heads(w, width):
    k = w.shape[0]
    w3 = w.reshape(k, MLA_HEADS, width)
    return jnp.pad(w3, ((0, 0), (0, 0), (0, LANES - width))).reshape(k, MLA_HEADS * LANES)


def _pack_table(tab):
    bits = lax.bitcast_convert_type(tab.astype(BF16), jnp.uint16).astype(jnp.uint32)
    packed = bits[:HALF_EXPERTS] | (bits[HALF_EXPERTS:] << 16)
    return packed.reshape(HALF_EXPERTS * SUBLANES, LANES)


def _group_matrices():
    e = (np.arange(RCOLS)[None, :] // GROUP == np.arange(PEER_PAIRS)[:, None])
    return jnp.asarray(e, BF16), jnp.asarray(e.T, BF16)


def _rope_tables(seq):
    half = MLA_ROPE // 2
    freqs = ROPE_THETA ** (-np.arange(half, dtype=np.float32) / half)
    ang = np.arange(seq, dtype=np.float32)[:, None] * freqs[None, :]
    cos = np.ones((seq, LANES), np.float32)
    sin = np.zeros((seq, LANES), np.float32)
    cos[:, MLA_NOPE:MLA_NOPE + half] = np.cos(ang)
    cos[:, MLA_NOPE + half:MLA_QK] = np.cos(ang)
    sin[:, MLA_NOPE:MLA_NOPE + half] = -np.sin(ang)
    sin[:, MLA_NOPE + half:MLA_QK] = np.sin(ang)
    return jnp.asarray(cos), jnp.asarray(sin)


def _tile(n, pref):
    return pref if n % pref == 0 else n


def _layer(x2, p2, layer_idx, batch, seq, g_mix, w_in, g_diff_q, g_diff_k, lam_q1, lam_k1, lam_q2,
           lam_k2, g_diff_sub, g_cq, w_uq, g_ckv, w_ukv, g_mla_q, g_mla_k, conv_w, w_br_diff,
           w_br_mla, w_br_conv, w_out, g_ffn, w_query, sub_keys, u_experts, v_experts, g_ple,
           w_ple_gate, w_ple_proj):
    n = x2.shape[0]
    lambda_init = 0.8 - 0.6 * math.exp(-0.3 * layer_idx)
    tm = _tile(seq, 256)
    tq = _tile(seq, 512)
    row = lambda v: v.reshape(1, -1)

    z2 = _inproj(x2, row(g_mix), _relayout_w_in(w_in), tm)
    z3 = z2.reshape(batch, seq, Z_W)

    slopes = jnp.asarray(2.0 ** (-8.0 * np.arange(1, DIFF_HEADS + 1, dtype=np.float32) / DIFF_HEADS))
    gq2 = jnp.tile(g_diff_q, 2).reshape(1, LANES)
    gk2 = jnp.tile(g_diff_k, 2).reshape(1, LANES)
    lam4 = jnp.stack([lam_q1, lam_k1, lam_q2, lam_k2])
    yd = _diff_attention(z3, slopes, gq2, gk2, lam4, row(g_diff_sub), tq, lambda_init)

    w_ukv3 = w_ukv.reshape(KV_LORA, MLA_HEADS, MLA_NOPE + MLA_V)
    wuk_p = _pad_heads(w_ukv3[:, :, :MLA_NOPE].reshape(KV_LORA, -1), MLA_NOPE).astype(BF16)
    wuv_p = w_ukv3[:, :, MLA_NOPE:].reshape(KV_LORA, -1).astype(BF16)
    wuq_p = _pad_heads(w_uq, MLA_QK).astype(BF16)
    gq_p = jnp.pad(g_mla_q, (0, LANES - MLA_QK)).reshape(1, LANES)
    gk_p = jnp.pad(g_mla_k, (0, LANES - MLA_QK)).reshape(1, LANES)
    cos_t, sin_t = _rope_tables(seq)
    qm, km, vm = _mla_proj(z2, row(g_cq), wuq_p, row(g_ckv), wuk_p, wuv_p, gq_p, gk_p,
                           cos_t, sin_t, tm, seq)
    ym = _mla_attention(qm.reshape(batch, seq, -1), km.reshape(batch, seq, -1),
                        vm.reshape(batch, seq, -1), tq)

    x2 = _merge(x2, yd.reshape(n, DIFF_W), ym.reshape(n, -1), z2, conv_w,
                w_br_diff.astype(BF16), w_br_mla.astype(BF16), w_br_conv.astype(BF16),
                w_out.astype(BF16), tm, seq)

    h2, idx8, hi, gate = _route(x2, row(g_ffn), w_query.astype(BF16), sub_keys.astype(BF16), tm)
    tb = _tile(n, PEER_TB)
    e_mat, g_mat = _group_matrices()
    w = _peer_u(idx8, _pack_table(u_experts), h2.reshape(n, SUBLANES, LANES), gate, hi,
                e_mat, g_mat, tb)
    peer = _peer_v(idx8, _pack_table(v_experts), w, hi, e_mat, tb)

    return _ple(x2, peer.reshape(n, D_MODEL), p2, row(g_ple), w_ple_gate.astype(BF16),
                w_ple_proj.astype(BF16), tm)


def kernel(x, p, g_mix, w_in, g_diff_q, g_diff_k, lam_q1, lam_k1, lam_q2, lam_k2, g_diff_sub, g_cq,
           w_uq, g_ckv, w_ukv, g_mla_q, g_mla_k, conv_w, w_br_diff, w_br_mla, w_br_conv, w_out,
           g_ffn, w_query, sub_keys, u_experts, v_experts, g_ple, w_ple_gate, w_ple_proj):
    batch, seq, _ = x.shape
    n = batch * seq
    x2 = x.reshape(n, D_MODEL)
    per_layer = (g_mix, w_in, g_diff_q, g_diff_k, lam_q1, lam_k1, lam_q2, lam_k2, g_diff_sub, g_cq,
                 w_uq, g_ckv, w_ukv, g_mla_q, g_mla_k, conv_w, w_br_diff, w_br_mla, w_br_conv,
                 w_out, g_ffn, w_query, sub_keys, u_experts, v_experts, g_ple, w_ple_gate,
                 w_ple_proj)
    for i in range(p.shape[0]):
        x2 = _layer(x2, p[i].reshape(n, PLE_DIM), i, batch, seq, *(a[i] for a in per_layer))
    return x2.reshape(batch, seq, D_MODEL)
```

```python
import functools
import math

import jax
import jax.numpy as jnp
import numpy as np
from jax import lax
from jax.experimental import pallas as pl
from jax.experimental.pallas import tpu as pltpu

F32 = jnp.float32
BF16 = jnp.bfloat16
I32 = jnp.int32

D_MODEL = 1024
DIFF_HEADS = 4
DIFF_HD = 64
DIFF_W = 512
MLA_HEADS = 8
MLA_NOPE = 64
MLA_ROPE = 32
MLA_V = 64
MLA_QK = 96
Q_LORA = 256
KV_LORA = 128
ROPE_THETA = 10000.0
CONV_W = 512
CONV_K = 3
PEER_HEADS = 8
N_KEYS = 128
N_EXPERTS = N_KEYS * N_KEYS
PEER_DK = 128
K_SUB = 16
PEER_TOPK = 16
PEER_PAIRS = PEER_HEADS * PEER_TOPK
PLE_DIM = 256
NORM_EPS = 1e-6
NEG_INF = -1e30

LANES = 128
SUBLANES = 8
HALF_TILE = SUBLANES // 2
VMEM_LIMIT = 56 * 1024 * 1024

Z_GATES = 0
Z_DQ = 3072
Z_DK = 3584
Z_DV = 4096
Z_CB = 4608
Z_CC = 5120
Z_CX = 5632
Z_MLA = 6144
Z_W = 6656


def _cparams(sem, vmem=VMEM_LIMIT):
    return pltpu.CompilerParams(dimension_semantics=sem, vmem_limit_bytes=vmem)


def _rms(x, g):
    return x * lax.rsqrt(jnp.mean(x * x, axis=-1, keepdims=True) + NORM_EPS) * g


def _inproj_kernel(x_ref, g_ref, w_ref, z_ref):
    h = _rms(x_ref[...], g_ref[...]).astype(BF16)
    z_ref[...] = jnp.dot(h, w_ref[...], preferred_element_type=F32)


def _inproj(x2, g, w_in_r, tm):
    n = x2.shape[0]
    return pl.pallas_call(
        _inproj_kernel,
        grid=(n // tm,),
        in_specs=[
            pl.BlockSpec((tm, D_MODEL), lambda i: (i, 0)),
            pl.BlockSpec((1, D_MODEL), lambda i: (0, 0)),
            pl.BlockSpec((D_MODEL, Z_W), lambda i: (0, 0)),
        ],
        out_specs=pl.BlockSpec((tm, Z_W), lambda i: (i, 0)),
        out_shape=jax.ShapeDtypeStruct((n, Z_W), F32),
        compiler_params=_cparams(("parallel",)),
        name="inproj",
    )(x2, g, w_in_r)


def _qk(q, kt):
    return lax.dot_general(q, kt, (((1,), (1,)), ((), ())), preferred_element_type=F32)


def _flash_update(s, m_ref, l_ref, acc_ref, vt, diag):
    tq, tk = s.shape
    if diag:
        row = lax.broadcasted_iota(I32, s.shape, 0)
        col = lax.broadcasted_iota(I32, s.shape, 1)
        s = jnp.where(row >= col, s, NEG_INF)
    m_old = m_ref[...]
    m_new = jnp.maximum(m_old, jnp.max(s, axis=-1, keepdims=True))
    p = jnp.exp(s - jnp.tile(m_new, (1, tk // LANES)))
    alpha = jnp.exp(m_old - m_new)
    l_ref[...] = alpha * l_ref[...] + jnp.sum(p, axis=-1, keepdims=True)
    m_ref[...] = m_new
    acc_ref[...] = alpha * acc_ref[...] + jnp.dot(p.astype(BF16), vt, preferred_element_type=F32)


def _flash_init(m_ref, l_ref, acc_ref):
    m_ref[...] = jnp.full(m_ref.shape, NEG_INF, F32)
    l_ref[...] = jnp.zeros(l_ref.shape, F32)
    acc_ref[...] = jnp.zeros(acc_ref.shape, F32)


def _flash_scratch(tq):
    return [pltpu.VMEM((tq, LANES), F32)] * 3


ALIBI_BITS = 6
ALIBI_SPLIT = 1 << ALIBI_BITS


def _diff_attn_kernel(slope_ref, q_ref, k_ref, v_ref, gq_ref, gk_ref, lam_ref, gsub_ref, o_ref,
                      k1_ref, k2_ref, m1_ref, l1_ref, a1_ref, m2_ref, l2_ref, a2_ref,
                      *, tq, seq, lambda_init):
    h = pl.program_id(1)
    qi = pl.program_id(2)
    lane = lax.broadcasted_iota(I32, (1, LANES), 1)
    first = lane < DIFF_HD
    slope = slope_ref[h]

    def halfnorm(x, g):
        x2 = x * x
        s_all = jnp.sum(x2, axis=-1, keepdims=True)
        s_lo = jnp.sum(jnp.where(first, x2, 0.0), axis=-1, keepdims=True)
        ms = jnp.where(first, s_lo, s_all - s_lo) * (1.0 / DIFF_HD)
        return x * lax.rsqrt(ms + NORM_EPS) * g

    def alibi_cols(pos0, base, query):
        pos = lax.broadcasted_iota(I32, (tq, 1), 0) + pos0
        hi = lax.shift_right_logical(pos, ALIBI_BITS).astype(F32) * (slope * ALIBI_SPLIT)
        lo = (pos & (ALIBI_SPLIT - 1)).astype(F32) * slope
        if query:
            c0, c1, c2, c3 = -hi, -lo, 1.0, 1.0
        else:
            c0, c1, c2, c3 = 1.0, 1.0, hi, lo
        return jnp.where(lane == base, c0, jnp.where(lane == base + 1, c1, jnp.where(
            lane == base + 2, c2, jnp.where(lane == base + 3, c3, 0.0))))

    @pl.when(qi == 0)
    def _():
        def body(c, carry):
            r0 = pl.multiple_of(c * tq, tq)
            kn = halfnorm(k_ref[pl.ds(r0, tq), :], gk_ref[...])
            k1_ref[pl.ds(r0, tq), :] = jnp.where(first, kn, alibi_cols(r0, DIFF_HD, False)).astype(BF16)
            k2_ref[pl.ds(r0, tq), :] = jnp.where(first, alibi_cols(r0, 0, False), kn).astype(BF16)
            return carry
        lax.fori_loop(0, seq // tq, body, 0)

    q = halfnorm(q_ref[...], gq_ref[...]) * (DIFF_HD ** -0.5)
    q1 = jnp.where(first, q, alibi_cols(qi * tq, DIFF_HD, True)).astype(BF16)
    q2 = jnp.where(first, alibi_cols(qi * tq, 0, True), q).astype(BF16)
    _flash_init(m1_ref, l1_ref, a1_ref)
    _flash_init(m2_ref, l2_ref, a2_ref)

    def step(j, diag):
        rows = pl.ds(pl.multiple_of(j * tq, tq), tq)
        vt = v_ref[rows, :].astype(BF16)
        _flash_update(_qk(q1, k1_ref[rows, :]), m1_ref, l1_ref, a1_ref, vt, diag)
        _flash_update(_qk(q2, k2_ref[rows, :]), m2_ref, l2_ref, a2_ref, vt, diag)

    def off_diag(j, carry):
        step(j, False)
        return carry

    lax.fori_loop(0, qi, off_diag, 0)
    step(qi, True)

    lam = lam_ref[...]
    e1 = jnp.exp(jnp.sum(lam[0:1] * lam[1:2], axis=-1, keepdims=True))
    e2 = jnp.exp(jnp.sum(lam[2:3] * lam[3:4], axis=-1, keepdims=True))
    lam_full = e1 - e2 + lambda_init
    o = a1_ref[...] / l1_ref[...] - lam_full * (a2_ref[...] / l2_ref[...])
    o_ref[...] = _rms(o, gsub_ref[...]) * (1.0 - lambda_init)


def _diff_attention(z3, slopes, gq2, gk2, lam4, gsub, tq, lambda_init):
    b, s, _ = z3.shape
    kern = functools.partial(_diff_attn_kernel, tq=tq, seq=s, lambda_init=lambda_init)
    return pl.pallas_call(
        kern,
        grid=(b, DIFF_HEADS, s // tq),
        in_specs=[
            pl.BlockSpec(memory_space=pltpu.SMEM),
            pl.BlockSpec((None, tq, LANES), lambda bi, h, qi: (bi, qi, Z_DQ // LANES + h)),
            pl.BlockSpec((None, s, LANES), lambda bi, h, qi: (bi, 0, Z_DK // LANES + h)),
            pl.BlockSpec((None, s, LANES), lambda bi, h, qi: (bi, 0, Z_DV // LANES + h)),
            pl.BlockSpec((1, LANES), lambda bi, h, qi: (0, 0)),
            pl.BlockSpec((1, LANES), lambda bi, h, qi: (0, 0)),
            pl.BlockSpec((4, DIFF_HD), lambda bi, h, qi: (0, 0)),
            pl.BlockSpec((1, LANES), lambda bi, h, qi: (0, 0)),
        ],
        out_specs=pl.BlockSpec((None, tq, LANES), lambda bi, h, qi: (bi, qi, h)),
        out_shape=jax.ShapeDtypeStruct((b, s, DIFF_W), F32),
        scratch_shapes=[pltpu.VMEM((s, LANES), BF16), pltpu.VMEM((s, LANES), BF16)]
        + _flash_scratch(tq) + _flash_scratch(tq),
        compiler_params=_cparams(("parallel", "parallel", "arbitrary")),
        name="diff_attn",
    )(slopes, z3, z3, z3, gq2, gk2, lam4, gsub)


def _mla_proj_kernel(z_ref, gcq_ref, wuq_ref, gckv_ref, wuk_ref, wuv_ref, gq_ref, gk_ref,
                     cos_ref, sin_ref, qm_ref, km_ref, vm_ref):
    z = z_ref[...]
    cq = z[:, :Q_LORA]
    ckv = z[:, Q_LORA:Q_LORA + KV_LORA]
    krb = z[:, Q_LORA + KV_LORA:]
    cqn = _rms(cq, gcq_ref[...]).astype(BF16)
    ckvn = _rms(ckv, gckv_ref[...]).astype(BF16)
    q = jnp.dot(cqn, wuq_ref[...], preferred_element_type=F32)
    kn = jnp.dot(ckvn, wuk_ref[...], preferred_element_type=F32)
    v = jnp.dot(ckvn, wuv_ref[...], preferred_element_type=F32)
    vm_ref[...] = v.astype(BF16)

    lane = lax.broadcasted_iota(I32, (1, LANES), 1)
    in_rope = (lane >= MLA_NOPE) & (lane < MLA_QK)
    in_x1 = (lane >= MLA_NOPE) & (lane < MLA_NOPE + MLA_ROPE // 2)
    kr = jnp.where(in_rope, pltpu.roll(krb, MLA_NOPE, 1), 0.0)
    cos = cos_ref[...]
    sin = sin_ref[...]
    gq = gq_ref[...]
    gk = gk_ref[...]
    half = MLA_ROPE // 2

    def norm_rope(xh, g):
        ms = jnp.sum(xh * xh, axis=-1, keepdims=True) * (1.0 / MLA_QK)
        xn = xh * lax.rsqrt(ms + NORM_EPS) * g
        sw = jnp.where(in_x1, pltpu.roll(xn, LANES - half, 1), pltpu.roll(xn, half, 1))
        return xn * cos + sw * sin

    for hh in range(MLA_HEADS):
        sl = slice(hh * LANES, (hh + 1) * LANES)
        qm_ref[:, sl] = (norm_rope(q[:, sl], gq) * (MLA_QK ** -0.5)).astype(BF16)
        km_ref[:, sl] = norm_rope(kn[:, sl] + kr, gk).astype(BF16)


def _mla_proj(z2, gcq, wuq_p, gckv, wuk_p, wuv_p, gq_p, gk_p, cos_t, sin_t, tm, seq):
    n = z2.shape[0]
    nsb = seq // tm
    full = lambda shp: pl.BlockSpec(shp, lambda i: (0, 0))
    return pl.pallas_call(
        _mla_proj_kernel,
        grid=(n // tm,),
        in_specs=[
            pl.BlockSpec((tm, 512), lambda i: (i, Z_MLA // 512)),
            full((1, Q_LORA)), full((Q_LORA, MLA_HEADS * LANES)),
            full((1, KV_LORA)), full((KV_LORA, MLA_HEADS * LANES)), full((KV_LORA, MLA_HEADS * MLA_V)),
            full((1, LANES)), full((1, LANES)),
            pl.BlockSpec((tm, LANES), lambda i: (i % nsb, 0)),
            pl.BlockSpec((tm, LANES), lambda i: (i % nsb, 0)),
        ],
        out_specs=[
            pl.BlockSpec((tm, MLA_HEADS * LANES), lambda i: (i, 0)),
            pl.BlockSpec((tm, MLA_HEADS * LANES), lambda i: (i, 0)),
            pl.BlockSpec((tm, MLA_HEADS * MLA_V), lambda i: (i, 0)),
        ],
        out_shape=[
            jax.ShapeDtypeStruct((n, MLA_HEADS * LANES), BF16),
            jax.ShapeDtypeStruct((n, MLA_HEADS * LANES), BF16),
            jax.ShapeDtypeStruct((n, MLA_HEADS * MLA_V), BF16),
        ],
        compiler_params=_cparams(("parallel",)),
        name="mla_proj",
    )(z2, gcq, wuq_p, gckv, wuk_p, wuv_p, gq_p, gk_p, cos_t, sin_t)


def _mla_attn_kernel(q_ref, k_ref, v_ref, o_ref, ma_ref, la_ref, aa_ref, mb_ref, lb_ref, ab_ref,
                     *, tq):
    qi = pl.program_id(2)
    _flash_init(ma_ref, la_ref, aa_ref)
    _flash_init(mb_ref, lb_ref, ab_ref)

    def step(j, diag):
        rows = pl.ds(pl.multiple_of(j * tq, tq), tq)
        vt = v_ref[rows, :]
        _flash_update(_qk(q_ref[:, :LANES], k_ref[rows, :LANES]), ma_ref, la_ref, aa_ref, vt, diag)
        _flash_update(_qk(q_ref[:, LANES:], k_ref[rows, LANES:]), mb_ref, lb_ref, ab_ref, vt, diag)

    def off_diag(j, carry):
        step(j, False)
        return carry

    lax.fori_loop(0, qi, off_diag, 0)
    step(qi, True)
    lane = lax.broadcasted_iota(I32, (1, LANES), 1)
    o_ref[...] = jnp.where(lane < MLA_V, aa_ref[...] / la_ref[...], ab_ref[...] / lb_ref[...])


def _mla_attention(qm3, km3, vm3, tq):
    b, s, _ = qm3.shape
    return pl.pallas_call(
        functools.partial(_mla_attn_kernel, tq=tq),
        grid=(b, MLA_HEADS // 2, s // tq),
        in_specs=[
            pl.BlockSpec((None, tq, 2 * LANES), lambda bi, hp, qi: (bi, qi, hp)),
            pl.BlockSpec((None, s, 2 * LANES), lambda bi, hp, qi: (bi, 0, hp)),
            pl.BlockSpec((None, s, LANES), lambda bi, hp, qi: (bi, 0, hp)),
        ],
        out_specs=pl.BlockSpec((None, tq, LANES), lambda bi, hp, qi: (bi, qi, hp)),
        out_shape=jax.ShapeDtypeStruct((b, s, MLA_HEADS * MLA_V), F32),
        scratch_shapes=_flash_scratch(tq) + _flash_scratch(tq),
        compiler_params=_cparams(("parallel", "parallel", "arbitrary")),
        name="mla_attn",
    )(qm3, km3, vm3)


def _merge_kernel(x_ref, yd_ref, ym_ref, cb_ref, cc_ref, cx_ref, cch_ref, cxh_ref,
                  g0_ref, g1_ref, g2_ref, cw_ref, wd_ref, wm_ref, wc_ref, wo_ref,
                  o_ref, ucat_ref, *, tm, seq):
    i = pl.program_id(0)
    at_seq_start = (i * tm) % seq == 0
    halo = cch_ref[...] * cxh_ref[...]
    ucat_ref[0:SUBLANES, :] = jnp.where(at_seq_start, 0.0, halo)
    ucat_ref[SUBLANES:, :] = cc_ref[...] * cx_ref[...]
    cw = cw_ref[...]
    conv = (cw[0:1] * ucat_ref[SUBLANES - 2:SUBLANES - 2 + tm, :]
            + cw[1:2] * ucat_ref[SUBLANES - 1:SUBLANES - 1 + tm, :]
            + cw[2:3] * ucat_ref[SUBLANES:, :])
    yc = cb_ref[...] * conv

    def br(y, w_ref, g_ref):
        return jax.nn.sigmoid(g_ref[...]) * jnp.dot(y.astype(BF16), w_ref[...],
                                                    preferred_element_type=F32)

    mixed = br(yd_ref[...], wd_ref, g0_ref) + br(ym_ref[...], wm_ref, g1_ref) + br(yc, wc_ref, g2_ref)
    o_ref[...] = x_ref[...] + jnp.dot(mixed.astype(BF16), wo_ref[...], preferred_element_type=F32)


def _merge(x2, yd2, ym2, z2, conv_w, wd, wm, wc, wo, tm, seq):
    n = x2.shape[0]
    hb = tm // SUBLANES
    full = lambda shp: pl.BlockSpec(shp, lambda i: (0, 0))
    zc = lambda off: pl.BlockSpec((tm, 512), lambda i: (i, off // 512))
    zh = lambda off: pl.BlockSpec((SUBLANES, 512), lambda i: (jnp.maximum(i * hb - 1, 0), off // 512))
    zg = lambda k: pl.BlockSpec((tm, D_MODEL), lambda i: (i, k))
    return pl.pallas_call(
        functools.partial(_merge_kernel, tm=tm, seq=seq),
        grid=(n // tm,),
        in_specs=[
            pl.BlockSpec((tm, D_MODEL), lambda i: (i, 0)),
            pl.BlockSpec((tm, DIFF_W), lambda i: (i, 0)),
            pl.BlockSpec((tm, 512), lambda i: (i, 0)),
            zc(Z_CB), zc(Z_CC), zc(Z_CX), zh(Z_CC), zh(Z_CX),
            zg(0), zg(1), zg(2),
            full((CONV_K, CONV_W)),
            full((DIFF_W, D_MODEL)), full((512, D_MODEL)), full((CONV_W, D_MODEL)),
            full((D_MODEL, D_MODEL)),
        ],
        out_specs=pl.BlockSpec((tm, D_MODEL), lambda i: (i, 0)),
        out_shape=jax.ShapeDtypeStruct((n, D_MODEL), F32),
        scratch_shapes=[pltpu.VMEM((tm + SUBLANES, CONV_W), F32)],
        compiler_params=_cparams(("parallel",)),
        name="merge",
    )(x2, yd2, ym2, z2, z2, z2, z2, z2, z2, z2, z2, conv_w, wd, wm, wc, wo)


ROUTE_TM = 512


def _topk_rows(s, k):
    rows, t = s.shape
    riota = lax.broadcasted_iota(I32, (rows, t), 0)
    kiota = lax.broadcasted_iota(I32, (k, t), 0)
    vals = jnp.zeros((k, t), F32)
    idxs = jnp.zeros((k, t), I32)
    for i in range(k):
        m = jnp.max(s, axis=0, keepdims=True)
        am = jnp.min(jnp.where(s == m, riota, rows), axis=0, keepdims=True)
        vals = jnp.where(kiota == i, m, vals)
        idxs = jnp.where(kiota == i, am, idxs)
        s = jnp.where(riota == am, -jnp.inf, s)
    return vals, idxs


def _route_kernel(x_ref, g_ref, wq_ref, keys_ref, h_ref, idx_ref, gate_ref,
                  q_ref, expt_ref, gatet_ref, *, tm):
    h2 = _rms(x_ref[...], g_ref[...])
    h_ref[...] = h2
    q_ref[...] = jnp.dot(h2.astype(BF16), wq_ref[...], preferred_element_type=F32)

    def head(hh, carry):
        tops = []
        for p in range(2):
            c0 = pl.multiple_of(hh * (2 * PEER_DK) + p * PEER_DK, PEER_DK)
            qhp = q_ref[:, pl.ds(c0, PEER_DK)].astype(BF16)
            keys = keys_ref[p, hh]
            st = _qk(keys, qhp)
            tops.append(_topk_rows(st, K_SUB))
        (s0, i0), (s1, i1) = tops
        nb = lambda a: K_SUB if a == 0 else SUBLANES
        cand = jnp.concatenate([s0[a:a + 1] + s1[:nb(a)] for a in range(K_SUB)], axis=0)
        cexp = jnp.concatenate([i0[a:a + 1] * N_KEYS + i1[:nb(a)] for a in range(K_SUB)], axis=0)
        nc = K_SUB + (K_SUB - 1) * SUBLANES
        riota = lax.broadcasted_iota(I32, (nc, tm), 0)
        kiota = lax.broadcasted_iota(I32, (PEER_TOPK, tm), 0)
        best = jnp.zeros((PEER_TOPK, tm), F32)
        bexp = jnp.zeros((PEER_TOPK, tm), I32)
        for i in range(PEER_TOPK):
            m = jnp.max(cand, axis=0, keepdims=True)
            am = jnp.min(jnp.where(cand == m, riota, nc), axis=0, keepdims=True)
            hit = riota == am
            e = jnp.max(jnp.where(hit, cexp, -1), axis=0, keepdims=True)
            best = jnp.where(kiota == i, m, best)
            bexp = jnp.where(kiota == i, e, bexp)
            cand = jnp.where(hit, -jnp.inf, cand)
        ex = jnp.exp(best - best[0:1])
        gate = ex / jnp.sum(ex, axis=0, keepdims=True)
        r0 = pl.multiple_of(hh * PEER_TOPK, PEER_TOPK)
        expt_ref[pl.ds(r0, PEER_TOPK), :] = bexp
        gatet_ref[pl.ds(r0, PEER_TOPK), :] = gate
        return carry

    lax.fori_loop(0, PEER_HEADS, head, 0)
    idx_ref[...] = expt_ref[...].T * HALF_TILE
    gate_ref[...] = gatet_ref[...].T


def _route(x2, g, wq, keys, tm):
    n = x2.shape[0]
    return pl.pallas_call(
        functools.partial(_route_kernel, tm=tm),
        grid=(n // tm,),
        in_specs=[
            pl.BlockSpec((tm, D_MODEL), lambda i: (i, 0)),
            pl.BlockSpec((1, D_MODEL), lambda i: (0, 0)),
            pl.BlockSpec((D_MODEL, 2 * PEER_DK * PEER_HEADS), lambda i: (0, 0)),
            pl.BlockSpec((2, PEER_HEADS, N_KEYS, PEER_DK), lambda i: (0, 0, 0, 0)),
        ],
        out_specs=[
            pl.BlockSpec((tm, D_MODEL), lambda i: (i, 0)),
            pl.BlockSpec((tm, PEER_PAIRS), lambda i: (i, 0)),
            pl.BlockSpec((tm, PEER_PAIRS), lambda i: (i, 0)),
        ],
        out_shape=[
            jax.ShapeDtypeStruct((n, D_MODEL), F32),
            jax.ShapeDtypeStruct((n, PEER_PAIRS), I32),
            jax.ShapeDtypeStruct((n, PEER_PAIRS), F32),
        ],
        scratch_shapes=[
            pltpu.VMEM((tm, 2 * PEER_DK * PEER_HEADS), F32),
            pltpu.VMEM((PEER_PAIRS, tm), I32),
            pltpu.VMEM((PEER_PAIRS, tm), F32),
        ],
        compiler_params=_cparams(("parallel",)),
        name="peer_route",
    )(x2, g, wq, keys)


RCOLS = PEER_PAIRS * SUBLANES
PEER_TB = 16


def _table_spec():
    return pl.BlockSpec((N_EXPERTS * HALF_TILE, LANES), lambda i: (0, 0),
                        pipeline_mode=pl.Buffered(1))


def _gather_rows(tab_ref, idx_ref, t):
    idx_t = idx_ref.at[t]
    tiles = [tab_ref[pl.ds(pl.multiple_of(idx_t[k], HALF_TILE), HALF_TILE), :]
             for k in range(PEER_PAIRS)]
    return pltpu.bitcast(jnp.concatenate(tiles, axis=0), BF16)


def _selector():
    s = lax.broadcasted_iota(I32, (SUBLANES, RCOLS), 0)
    c = lax.broadcasted_iota(I32, (SUBLANES, RCOLS), 1)
    return (c & (SUBLANES - 1)) == s


def _peer_u_kernel(idx_ref, tab_ref, h_ref, gate_ref, g_ref, w_ref, res_ref, *, tb):
    keep = _selector()

    for t in range(tb):
        r = _gather_rows(tab_ref, idx_ref, t)
        res = _qk(h_ref[t].astype(BF16), r)
        res_ref[t] = jnp.where(keep, res, 0.0)
    res = res_ref[...].reshape(tb * SUBLANES, RCOLS)
    res_hi = res.astype(BF16)
    res_lo = (res - res_hi.astype(F32)).astype(BF16)
    part = (jnp.dot(res_hi, g_ref[...], preferred_element_type=F32)
            + jnp.dot(res_lo, g_ref[...], preferred_element_type=F32))
    act = jnp.sum(part.reshape(tb, SUBLANES, PEER_PAIRS), axis=1)
    gelu = 0.5 * act * (1.0 + lax.erf(act * (2.0 ** -0.5)))
    w_ref[...] = gate_ref[...] * gelu


def _peer_u(idx, tab, h3, gate, g_mat, tb):
    n = idx.shape[0]
    return pl.pallas_call(
        functools.partial(_peer_u_kernel, tb=tb),
        grid=(n // tb,),
        in_specs=[
            pl.BlockSpec((tb, PEER_PAIRS), lambda i: (i, 0), memory_space=pltpu.SMEM),
            _table_spec(),
            pl.BlockSpec((tb, SUBLANES, LANES), lambda i: (i, 0, 0)),
            pl.BlockSpec((tb, PEER_PAIRS), lambda i: (i, 0)),
            pl.BlockSpec((RCOLS, PEER_PAIRS), lambda i: (0, 0)),
        ],
        out_specs=pl.BlockSpec((tb, PEER_PAIRS), lambda i: (i, 0)),
        out_shape=jax.ShapeDtypeStruct((n, PEER_PAIRS), F32),
        scratch_shapes=[pltpu.VMEM((tb, SUBLANES, RCOLS), F32)],
        compiler_params=_cparams(("arbitrary",)),
        name="peer_u",
    )(idx, tab, h3, gate, g_mat)


def _peer_v_kernel(idx_ref, tab_ref, w_ref, e_ref, o_ref, wrep_ref, *, tb):
    wrep_ref[...] = jnp.dot(w_ref[...].astype(BF16), e_ref[...], preferred_element_type=F32)
    keep = _selector()

    for t in range(tb):
        r = _gather_rows(tab_ref, idx_ref, t)
        sel = jnp.where(keep, wrep_ref[pl.ds(t, 1), :], 0.0)
        o_ref[t] = jnp.dot(sel.astype(BF16), r, preferred_element_type=F32)


def _peer_v(idx, tab, w, e_mat, tb):
    n = idx.shape[0]
    return pl.pallas_call(
        functools.partial(_peer_v_kernel, tb=tb),
        grid=(n // tb,),
        in_specs=[
            pl.BlockSpec((tb, PEER_PAIRS), lambda i: (i, 0), memory_space=pltpu.SMEM),
            _table_spec(),
            pl.BlockSpec((tb, PEER_PAIRS), lambda i: (i, 0)),
            pl.BlockSpec((PEER_PAIRS, RCOLS), lambda i: (0, 0)),
        ],
        out_specs=pl.BlockSpec((tb, SUBLANES, LANES), lambda i: (i, 0, 0)),
        out_shape=jax.ShapeDtypeStruct((n, SUBLANES, LANES), F32),
        scratch_shapes=[pltpu.VMEM((tb, RCOLS), F32)],
        compiler_params=_cparams(("arbitrary",)),
        name="peer_v",
    )(idx, tab, w, e_mat)


def _ple_kernel(x_ref, peer_ref, p_ref, g_ref, wg_ref, wp_ref, o_ref):
    x = x_ref[...] + peer_ref[...]
    gate = jax.nn.sigmoid(jnp.dot(_rms(x, g_ref[...]).astype(BF16), wg_ref[...],
                                  preferred_element_type=F32))
    emb = jnp.dot(p_ref[...].astype(BF16), wp_ref[...], preferred_element_type=F32)
    o_ref[...] = x + gate * emb


def _ple(x2, peer2, p2, g, wg, wp, tm):
    n = x2.shape[0]
    full = lambda shp: pl.BlockSpec(shp, lambda i: (0, 0))
    return pl.pallas_call(
        _ple_kernel,
        grid=(n // tm,),
        in_specs=[
            pl.BlockSpec((tm, D_MODEL), lambda i: (i, 0)),
            pl.BlockSpec((tm, D_MODEL), lambda i: (i, 0)),
            pl.BlockSpec((tm, PLE_DIM), lambda i: (i, 0)),
            full((1, D_MODEL)), full((D_MODEL, D_MODEL)), full((PLE_DIM, D_MODEL)),
        ],
        out_specs=pl.BlockSpec((tm, D_MODEL), lambda i: (i, 0)),
        out_shape=jax.ShapeDtypeStruct((n, D_MODEL), F32),
        compiler_params=_cparams(("parallel",)),
        name="ple",
    )(x2, peer2, p2, g, wg, wp)


def _relayout_w_in(w_in):
    o = np.cumsum([0, 512, 512, 512, Q_LORA, KV_LORA, MLA_ROPE, 512, 512, 512, 3 * D_MODEL])
    seg = lambda k: w_in[:, o[k]:o[k + 1]]
    pad = jnp.zeros((D_MODEL, 512 - Q_LORA - KV_LORA - MLA_ROPE), w_in.dtype)
    return jnp.concatenate([seg(9), seg(0), seg(1), seg(2), seg(6), seg(7), seg(8),
                            seg(3), seg(4), seg(5), pad], axis=1).astype(BF16)


def _pad_heads(w, width):
    k = w.shape[0]
    w3 = w.reshape(k, MLA_HEADS, width)
    return jnp.pad(w3, ((0, 0), (0, 0), (0, LANES - width))).reshape(k, MLA_HEADS * LANES)


def _pack_table(tab):
    bits = lax.bitcast_convert_type(tab.astype(BF16), jnp.uint16).astype(jnp.uint32)
    bits = bits.reshape(N_EXPERTS, 2, HALF_TILE, LANES)
    return (bits[:, 0] | (bits[:, 1] << 16)).reshape(N_EXPERTS * HALF_TILE, LANES)


def _to_packed_rows(v, n):
    return v.reshape(n, 2, HALF_TILE, LANES).transpose(0, 2, 1, 3).reshape(n, SUBLANES, LANES)


def _from_packed_rows(v, n):
    return v.reshape(n, HALF_TILE, 2, LANES).transpose(0, 2, 1, 3).reshape(n, D_MODEL)


def _group_matrices():
    e = (np.arange(RCOLS)[None, :] // SUBLANES == np.arange(PEER_PAIRS)[:, None])
    return jnp.asarray(e, BF16), jnp.asarray(e.T, BF16)


def _rope_tables(seq):
    half = MLA_ROPE // 2
    freqs = ROPE_THETA ** (-np.arange(half, dtype=np.float32) / half)
    ang = np.arange(seq, dtype=np.float32)[:, None] * freqs[None, :]
    cos = np.ones((seq, LANES), np.float32)
    sin = np.zeros((seq, LANES), np.float32)
    cos[:, MLA_NOPE:MLA_NOPE + half] = np.cos(ang)
    cos[:, MLA_NOPE + half:MLA_QK] = np.cos(ang)
    sin[:, MLA_NOPE:MLA_NOPE + half] = -np.sin(ang)
    sin[:, MLA_NOPE + half:MLA_QK] = np.sin(ang)
    return jnp.asarray(cos), jnp.asarray(sin)


def _tile(n, pref):
    return pref if n % pref == 0 else n


def _layer(x2, p2, layer_idx, batch, seq, g_mix, w_in, g_diff_q, g_diff_k, lam_q1, lam_k1, lam_q2,
           lam_k2, g_diff_sub, g_cq, w_uq, g_ckv, w_ukv, g_mla_q, g_mla_k, conv_w, w_br_diff,
           w_br_mla, w_br_conv, w_out, g_ffn, w_query, sub_keys, u_experts, v_experts, g_ple,
           w_ple_gate, w_ple_proj):
    n = x2.shape[0]
    lambda_init = 0.8 - 0.6 * math.exp(-0.3 * layer_idx)
    tm = _tile(seq, 256)
    tq = _tile(seq, 512)
    row = lambda v: v.reshape(1, -1)

    z2 = _inproj(x2, row(g_mix), _relayout_w_in(w_in), tm)
    z3 = z2.reshape(batch, seq, Z_W)

    slopes = jnp.asarray(2.0 ** (-8.0 * np.arange(1, DIFF_HEADS + 1, dtype=np.float32) / DIFF_HEADS))
    gq2 = jnp.tile(g_diff_q, 2).reshape(1, LANES)
    gk2 = jnp.tile(g_diff_k, 2).reshape(1, LANES)
    lam4 = jnp.stack([lam_q1, lam_k1, lam_q2, lam_k2])
    yd = _diff_attention(z3, slopes, gq2, gk2, lam4, row(g_diff_sub), tq, lambda_init)

    w_ukv3 = w_ukv.reshape(KV_LORA, MLA_HEADS, MLA_NOPE + MLA_V)
    wuk_p = _pad_heads(w_ukv3[:, :, :MLA_NOPE].reshape(KV_LORA, -1), MLA_NOPE).astype(BF16)
    wuv_p = w_ukv3[:, :, MLA_NOPE:].reshape(KV_LORA, -1).astype(BF16)
    wuq_p = _pad_heads(w_uq, MLA_QK).astype(BF16)
    gq_p = jnp.pad(g_mla_q, (0, LANES - MLA_QK)).reshape(1, LANES)
    gk_p = jnp.pad(g_mla_k, (0, LANES - MLA_QK)).reshape(1, LANES)
    cos_t, sin_t = _rope_tables(seq)
    qm, km, vm = _mla_proj(z2, row(g_cq), wuq_p, row(g_ckv), wuk_p, wuv_p, gq_p, gk_p,
                           cos_t, sin_t, tm, seq)
    ym = _mla_attention(qm.reshape(batch, seq, -1), km.reshape(batch, seq, -1),
                        vm.reshape(batch, seq, -1), tq)

    x2 = _merge(x2, yd.reshape(n, DIFF_W), ym.reshape(n, -1), z2, conv_w,
                w_br_diff.astype(BF16), w_br_mla.astype(BF16), w_br_conv.astype(BF16),
                w_out.astype(BF16), tm, seq)

    h2, idx, gate = _route(x2, row(g_ffn), w_query.astype(BF16), sub_keys.astype(BF16),
                           _tile(seq, ROUTE_TM))
    tb = _tile(n, PEER_TB)
    e_mat, g_mat = _group_matrices()
    w = _peer_u(idx, _pack_table(u_experts), _to_packed_rows(h2, n), gate, g_mat, tb)
    peer = _peer_v(idx, _pack_table(v_experts), w, e_mat, tb)

    return _ple(x2, _from_packed_rows(peer, n), p2, row(g_ple), w_ple_gate.astype(BF16),
                w_ple_proj.astype(BF16), tm)


def kernel(x, p, g_mix, w_in, g_diff_q, g_diff_k, lam_q1, lam_k1, lam_q2, lam_k2, g_diff_sub, g_cq,
           w_uq, g_ckv, w_ukv, g_mla_q, g_mla_k, conv_w, w_br_diff, w_br_mla, w_br_conv, w_out,
           g_ffn, w_query, sub_keys, u_experts, v_experts, g_ple, w_ple_gate, w_ple_proj):
    batch, seq, _ = x.shape
    n = batch * seq
    x2 = x.reshape(n, D_MODEL)
    per_layer = (g_mix, w_in, g_diff_q, g_diff_k, lam_q1, lam_k1, lam_q2, lam_k2, g_diff_sub, g_cq,
                 w_uq, g_ckv, w_ukv, g_mla_q, g_mla_k, conv_w, w_br_diff, w_br_mla, w_br_conv,
                 w_out, g_ffn, w_query, sub_keys, u_experts, v_experts, g_ple, w_ple_gate,
                 w_ple_proj)
    for i in range(p.shape[0]):
        x2 = _layer(x2, p[i].reshape(n, PLE_DIM), i, batch, seq, *(a[i] for a in per_layer))
    return x2.reshape(batch, seq, D_MODEL)
```

```python
import functools
import math

import jax
import jax.numpy as jnp
import numpy as np
from jax import lax
from jax.experimental import pallas as pl
from jax.experimental.pallas import tpu as pltpu

F32 = jnp.float32
BF16 = jnp.bfloat16
I32 = jnp.int32

D_MODEL = 1024
DIFF_HEADS = 4
DIFF_HD = 64
DIFF_W = 512
MLA_HEADS = 8
MLA_NOPE = 64
MLA_ROPE = 32
MLA_V = 64
MLA_QK = 96
Q_LORA = 256
KV_LORA = 128
ROPE_THETA = 10000.0
CONV_W = 512
CONV_K = 3
PEER_HEADS = 8
N_KEYS = 128
N_EXPERTS = N_KEYS * N_KEYS
PEER_DK = 128
K_SUB = 16
PEER_TOPK = 16
PEER_PAIRS = PEER_HEADS * PEER_TOPK
PLE_DIM = 256
NORM_EPS = 1e-6
NEG_INF = -1e30

LANES = 128
SUBLANES = 8
HALF_TILE = SUBLANES // 2
VMEM_LIMIT = 56 * 1024 * 1024

Z_GATES = 0
Z_DQ = 3072
Z_DK = 3584
Z_DV = 4096
Z_CB = 4608
Z_CC = 5120
Z_CX = 5632
Z_MLA = 6144
Z_W = 6656


def _cparams(sem, vmem=VMEM_LIMIT):
    return pltpu.CompilerParams(dimension_semantics=sem, vmem_limit_bytes=vmem)


def _rms(x, g):
    return x * lax.rsqrt(jnp.mean(x * x, axis=-1, keepdims=True) + NORM_EPS) * g


def _inproj_kernel(x_ref, g_ref, w_ref, z_ref):
    h = _rms(x_ref[...], g_ref[...]).astype(BF16)
    z_ref[...] = jnp.dot(h, w_ref[...], preferred_element_type=F32)


def _inproj(x2, g, w_in_r, tm):
    n = x2.shape[0]
    return pl.pallas_call(
        _inproj_kernel,
        grid=(n // tm,),
        in_specs=[
            pl.BlockSpec((tm, D_MODEL), lambda i: (i, 0)),
            pl.BlockSpec((1, D_MODEL), lambda i: (0, 0)),
            pl.BlockSpec((D_MODEL, Z_W), lambda i: (0, 0)),
        ],
        out_specs=pl.BlockSpec((tm, Z_W), lambda i: (i, 0)),
        out_shape=jax.ShapeDtypeStruct((n, Z_W), F32),
        compiler_params=_cparams(("parallel",)),
        name="inproj",
    )(x2, g, w_in_r)


def _qk(q, kt):
    return lax.dot_general(q, kt, (((1,), (1,)), ((), ())), preferred_element_type=F32)


def _flash_update(s, m_ref, l_ref, acc_ref, vt, diag):
    tq, tk = s.shape
    if diag:
        row = lax.broadcasted_iota(I32, s.shape, 0)
        col = lax.broadcasted_iota(I32, s.shape, 1)
        s = jnp.where(row >= col, s, NEG_INF)
    m_old = m_ref[...]
    m_new = jnp.maximum(m_old, jnp.max(s, axis=-1, keepdims=True))
    p = jnp.exp(s - jnp.tile(m_new, (1, tk // LANES)))
    alpha = jnp.exp(m_old - m_new)
    l_ref[...] = alpha * l_ref[...] + jnp.sum(p, axis=-1, keepdims=True)
    m_ref[...] = m_new
    acc_ref[...] = alpha * acc_ref[...] + jnp.dot(p.astype(BF16), vt, preferred_element_type=F32)


def _flash_init(m_ref, l_ref, acc_ref):
    m_ref[...] = jnp.full(m_ref.shape, NEG_INF, F32)
    l_ref[...] = jnp.zeros(l_ref.shape, F32)
    acc_ref[...] = jnp.zeros(acc_ref.shape, F32)


def _flash_scratch(tq):
    return [pltpu.VMEM((tq, LANES), F32)] * 3


ALIBI_BITS = 6
ALIBI_SPLIT = 1 << ALIBI_BITS


def _diff_attn_kernel(slope_ref, q_ref, k_ref, v_ref, gq_ref, gk_ref, lam_ref, gsub_ref, o_ref,
                      k1_ref, k2_ref, m1_ref, l1_ref, a1_ref, m2_ref, l2_ref, a2_ref,
                      *, tq, seq, lambda_init):
    h = pl.program_id(1)
    qi = pl.program_id(2)
    lane = lax.broadcasted_iota(I32, (1, LANES), 1)
    first = lane < DIFF_HD
    slope = slope_ref[h]

    def halfnorm(x, g):
        x2 = x * x
        s_all = jnp.sum(x2, axis=-1, keepdims=True)
        s_lo = jnp.sum(jnp.where(first, x2, 0.0), axis=-1, keepdims=True)
        ms = jnp.where(first, s_lo, s_all - s_lo) * (1.0 / DIFF_HD)
        return x * lax.rsqrt(ms + NORM_EPS) * g

    def alibi_cols(pos0, base, query):
        pos = lax.broadcasted_iota(I32, (tq, 1), 0) + pos0
        hi = lax.shift_right_logical(pos, ALIBI_BITS).astype(F32) * (slope * ALIBI_SPLIT)
        lo = (pos & (ALIBI_SPLIT - 1)).astype(F32) * slope
        if query:
            c0, c1, c2, c3 = -hi, -lo, 1.0, 1.0
        else:
            c0, c1, c2, c3 = 1.0, 1.0, hi, lo
        return jnp.where(lane == base, c0, jnp.where(lane == base + 1, c1, jnp.where(
            lane == base + 2, c2, jnp.where(lane == base + 3, c3, 0.0))))

    @pl.when(qi == 0)
    def _():
        def body(c, carry):
            r0 = pl.multiple_of(c * tq, tq)
            kn = halfnorm(k_ref[pl.ds(r0, tq), :], gk_ref[...])
            k1_ref[pl.ds(r0, tq), :] = jnp.where(first, kn, alibi_cols(r0, DIFF_HD, False)).astype(BF16)
            k2_ref[pl.ds(r0, tq), :] = jnp.where(first, alibi_cols(r0, 0, False), kn).astype(BF16)
            return carry
        lax.fori_loop(0, seq // tq, body, 0)

    q = halfnorm(q_ref[...], gq_ref[...]) * (DIFF_HD ** -0.5)
    q1 = jnp.where(first, q, alibi_cols(qi * tq, DIFF_HD, True)).astype(BF16)
    q2 = jnp.where(first, alibi_cols(qi * tq, 0, True), q).astype(BF16)
    _flash_init(m1_ref, l1_ref, a1_ref)
    _flash_init(m2_ref, l2_ref, a2_ref)

    def step(j, diag):
        rows = pl.ds(pl.multiple_of(j * tq, tq), tq)
        vt = v_ref[rows, :].astype(BF16)
        _flash_update(_qk(q1, k1_ref[rows, :]), m1_ref, l1_ref, a1_ref, vt, diag)
        _flash_update(_qk(q2, k2_ref[rows, :]), m2_ref, l2_ref, a2_ref, vt, diag)

    def off_diag(j, carry):
        step(j, False)
        return carry

    lax.fori_loop(0, qi, off_diag, 0)
    step(qi, True)

    lam = lam_ref[...]
    e1 = jnp.exp(jnp.sum(lam[0:1] * lam[1:2], axis=-1, keepdims=True))
    e2 = jnp.exp(jnp.sum(lam[2:3] * lam[3:4], axis=-1, keepdims=True))
    lam_full = e1 - e2 + lambda_init
    o = a1_ref[...] / l1_ref[...] - lam_full * (a2_ref[...] / l2_ref[...])
    o_ref[...] = _rms(o, gsub_ref[...]) * (1.0 - lambda_init)


def _diff_attention(z3, slopes, gq2, gk2, lam4, gsub, tq, lambda_init):
    b, s, _ = z3.shape
    kern = functools.partial(_diff_attn_kernel, tq=tq, seq=s, lambda_init=lambda_init)
    return pl.pallas_call(
        kern,
        grid=(b, DIFF_HEADS, s // tq),
        in_specs=[
            pl.BlockSpec(memory_space=pltpu.SMEM),
            pl.BlockSpec((None, tq, LANES), lambda bi, h, qi: (bi, qi, Z_DQ // LANES + h)),
            pl.BlockSpec((None, s, LANES), lambda bi, h, qi: (bi, 0, Z_DK // LANES + h)),
            pl.BlockSpec((None, s, LANES), lambda bi, h, qi: (bi, 0, Z_DV // LANES + h)),
            pl.BlockSpec((1, LANES), lambda bi, h, qi: (0, 0)),
            pl.BlockSpec((1, LANES), lambda bi, h, qi: (0, 0)),
            pl.BlockSpec((4, DIFF_HD), lambda bi, h, qi: (0, 0)),
            pl.BlockSpec((1, LANES), lambda bi, h, qi: (0, 0)),
        ],
        out_specs=pl.BlockSpec((None, tq, LANES), lambda bi, h, qi: (bi, qi, h)),
        out_shape=jax.ShapeDtypeStruct((b, s, DIFF_W), F32),
        scratch_shapes=[pltpu.VMEM((s, LANES), BF16), pltpu.VMEM((s, LANES), BF16)]
        + _flash_scratch(tq) + _flash_scratch(tq),
        compiler_params=_cparams(("parallel", "parallel", "arbitrary")),
        name="diff_attn",
    )(slopes, z3, z3, z3, gq2, gk2, lam4, gsub)


def _mla_proj_kernel(z_ref, gcq_ref, wuq_ref, gckv_ref, wuk_ref, wuv_ref, gq_ref, gk_ref,
                     cos_ref, sin_ref, qm_ref, km_ref, vm_ref):
    z = z_ref[...]
    cq = z[:, :Q_LORA]
    ckv = z[:, Q_LORA:Q_LORA + KV_LORA]
    krb = z[:, Q_LORA + KV_LORA:]
    cqn = _rms(cq, gcq_ref[...]).astype(BF16)
    ckvn = _rms(ckv, gckv_ref[...]).astype(BF16)
    q = jnp.dot(cqn, wuq_ref[...], preferred_element_type=F32)
    kn = jnp.dot(ckvn, wuk_ref[...], preferred_element_type=F32)
    v = jnp.dot(ckvn, wuv_ref[...], preferred_element_type=F32)
    vm_ref[...] = v.astype(BF16)

    lane = lax.broadcasted_iota(I32, (1, LANES), 1)
    in_rope = (lane >= MLA_NOPE) & (lane < MLA_QK)
    in_x1 = (lane >= MLA_NOPE) & (lane < MLA_NOPE + MLA_ROPE // 2)
    kr = jnp.where(in_rope, pltpu.roll(krb, MLA_NOPE, 1), 0.0)
    cos = cos_ref[...]
    sin = sin_ref[...]
    gq = gq_ref[...]
    gk = gk_ref[...]
    half = MLA_ROPE // 2

    def norm_rope(xh, g):
        ms = jnp.sum(xh * xh, axis=-1, keepdims=True) * (1.0 / MLA_QK)
        xn = xh * lax.rsqrt(ms + NORM_EPS) * g
        sw = jnp.where(in_x1, pltpu.roll(xn, LANES - half, 1), pltpu.roll(xn, half, 1))
        return xn * cos + sw * sin

    for hh in range(MLA_HEADS):
        sl = slice(hh * LANES, (hh + 1) * LANES)
        qm_ref[:, sl] = (norm_rope(q[:, sl], gq) * (MLA_QK ** -0.5)).astype(BF16)
        km_ref[:, sl] = norm_rope(kn[:, sl] + kr, gk).astype(BF16)


def _mla_proj(z2, gcq, wuq_p, gckv, wuk_p, wuv_p, gq_p, gk_p, cos_t, sin_t, tm, seq):
    n = z2.shape[0]
    nsb = seq // tm
    full = lambda shp: pl.BlockSpec(shp, lambda i: (0, 0))
    return pl.pallas_call(
        _mla_proj_kernel,
        grid=(n // tm,),
        in_specs=[
            pl.BlockSpec((tm, 512), lambda i: (i, Z_MLA // 512)),
            full((1, Q_LORA)), full((Q_LORA, MLA_HEADS * LANES)),
            full((1, KV_LORA)), full((KV_LORA, MLA_HEADS * LANES)), full((KV_LORA, MLA_HEADS * MLA_V)),
            full((1, LANES)), full((1, LANES)),
            pl.BlockSpec((tm, LANES), lambda i: (i % nsb, 0)),
            pl.BlockSpec((tm, LANES), lambda i: (i % nsb, 0)),
        ],
        out_specs=[
            pl.BlockSpec((tm, MLA_HEADS * LANES), lambda i: (i, 0)),
            pl.BlockSpec((tm, MLA_HEADS * LANES), lambda i: (i, 0)),
            pl.BlockSpec((tm, MLA_HEADS * MLA_V), lambda i: (i, 0)),
        ],
        out_shape=[
            jax.ShapeDtypeStruct((n, MLA_HEADS * LANES), BF16),
            jax.ShapeDtypeStruct((n, MLA_HEADS * LANES), BF16),
            jax.ShapeDtypeStruct((n, MLA_HEADS * MLA_V), BF16),
        ],
        compiler_params=_cparams(("parallel",)),
        name="mla_proj",
    )(z2, gcq, wuq_p, gckv, wuk_p, wuv_p, gq_p, gk_p, cos_t, sin_t)


def _mla_attn_kernel(q_ref, k_ref, v_ref, o_ref, ma_ref, la_ref, aa_ref, mb_ref, lb_ref, ab_ref,
                     *, tq):
    qi = pl.program_id(2)
    _flash_init(ma_ref, la_ref, aa_ref)
    _flash_init(mb_ref, lb_ref, ab_ref)

    def step(j, diag):
        rows = pl.ds(pl.multiple_of(j * tq, tq), tq)
        vt = v_ref[rows, :]
        _flash_update(_qk(q_ref[:, :LANES], k_ref[rows, :LANES]), ma_ref, la_ref, aa_ref, vt, diag)
        _flash_update(_qk(q_ref[:, LANES:], k_ref[rows, LANES:]), mb_ref, lb_ref, ab_ref, vt, diag)

    def off_diag(j, carry):
        step(j, False)
        return carry

    lax.fori_loop(0, qi, off_diag, 0)
    step(qi, True)
    lane = lax.broadcasted_iota(I32, (1, LANES), 1)
    o_ref[...] = jnp.where(lane < MLA_V, aa_ref[...] / la_ref[...], ab_ref[...] / lb_ref[...])


def _mla_attention(qm3, km3, vm3, tq):
    b, s, _ = qm3.shape
    return pl.pallas_call(
        functools.partial(_mla_attn_kernel, tq=tq),
        grid=(b, MLA_HEADS // 2, s // tq),
        in_specs=[
            pl.BlockSpec((None, tq, 2 * LANES), lambda bi, hp, qi: (bi, qi, hp)),
            pl.BlockSpec((None, s, 2 * LANES), lambda bi, hp, qi: (bi, 0, hp)),
            pl.BlockSpec((None, s, LANES), lambda bi, hp, qi: (bi, 0, hp)),
        ],
        out_specs=pl.BlockSpec((None, tq, LANES), lambda bi, hp, qi: (bi, qi, hp)),
        out_shape=jax.ShapeDtypeStruct((b, s, MLA_HEADS * MLA_V), F32),
        scratch_shapes=_flash_scratch(tq) + _flash_scratch(tq),
        compiler_params=_cparams(("parallel", "parallel", "arbitrary")),
        name="mla_attn",
    )(qm3, km3, vm3)


def _merge_kernel(x_ref, yd_ref, ym_ref, cb_ref, cc_ref, cx_ref, cch_ref, cxh_ref,
                  g0_ref, g1_ref, g2_ref, cw_ref, wd_ref, wm_ref, wc_ref, wo_ref,
                  o_ref, ucat_ref, *, tm, seq):
    i = pl.program_id(0)
    at_seq_start = (i * tm) % seq == 0
    halo = cch_ref[...] * cxh_ref[...]
    ucat_ref[0:SUBLANES, :] = jnp.where(at_seq_start, 0.0, halo)
    ucat_ref[SUBLANES:, :] = cc_ref[...] * cx_ref[...]
    cw = cw_ref[...]
    conv = (cw[0:1] * ucat_ref[SUBLANES - 2:SUBLANES - 2 + tm, :]
            + cw[1:2] * ucat_ref[SUBLANES - 1:SUBLANES - 1 + tm, :]
            + cw[2:3] * ucat_ref[SUBLANES:, :])
    yc = cb_ref[...] * conv

    def br(y, w_ref, g_ref):
        return jax.nn.sigmoid(g_ref[...]) * jnp.dot(y.astype(BF16), w_ref[...],
                                                    preferred_element_type=F32)

    mixed = br(yd_ref[...], wd_ref, g0_ref) + br(ym_ref[...], wm_ref, g1_ref) + br(yc, wc_ref, g2_ref)
    o_ref[...] = x_ref[...] + jnp.dot(mixed.astype(BF16), wo_ref[...], preferred_element_type=F32)


def _merge(x2, yd2, ym2, z2, conv_w, wd, wm, wc, wo, tm, seq):
    n = x2.shape[0]
    hb = tm // SUBLANES
    full = lambda shp: pl.BlockSpec(shp, lambda i: (0, 0))
    zc = lambda off: pl.BlockSpec((tm, 512), lambda i: (i, off // 512))
    zh = lambda off: pl.BlockSpec((SUBLANES, 512), lambda i: (jnp.maximum(i * hb - 1, 0), off // 512))
    zg = lambda k: pl.BlockSpec((tm, D_MODEL), lambda i: (i, k))
    return pl.pallas_call(
        functools.partial(_merge_kernel, tm=tm, seq=seq),
        grid=(n // tm,),
        in_specs=[
            pl.BlockSpec((tm, D_MODEL), lambda i: (i, 0)),
            pl.BlockSpec((tm, DIFF_W), lambda i: (i, 0)),
            pl.BlockSpec((tm, 512), lambda i: (i, 0)),
            zc(Z_CB), zc(Z_CC), zc(Z_CX), zh(Z_CC), zh(Z_CX),
            zg(0), zg(1), zg(2),
            full((CONV_K, CONV_W)),
            full((DIFF_W, D_MODEL)), full((512, D_MODEL)), full((CONV_W, D_MODEL)),
            full((D_MODEL, D_MODEL)),
        ],
        out_specs=pl.BlockSpec((tm, D_MODEL), lambda i: (i, 0)),
        out_shape=jax.ShapeDtypeStruct((n, D_MODEL), F32),
        scratch_shapes=[pltpu.VMEM((tm + SUBLANES, CONV_W), F32)],
        compiler_params=_cparams(("parallel",)),
        name="merge",
    )(x2, yd2, ym2, z2, z2, z2, z2, z2, z2, z2, z2, conv_w, wd, wm, wc, wo)


ROUTE_TM = 512


def _topk_rows(s, k):
    rows, t = s.shape
    riota = lax.broadcasted_iota(I32, (rows, t), 0)
    kiota = lax.broadcasted_iota(I32, (k, t), 0)
    vals = jnp.zeros((k, t), F32)
    idxs = jnp.zeros((k, t), I32)
    for i in range(k):
        m = jnp.max(s, axis=0, keepdims=True)
        am = jnp.min(jnp.where(s == m, riota, rows), axis=0, keepdims=True)
        vals = jnp.where(kiota == i, m, vals)
        idxs = jnp.where(kiota == i, am, idxs)
        s = jnp.where(riota == am, -jnp.inf, s)
    return vals, idxs


def _route_kernel(x_ref, g_ref, wq_ref, keys_ref, h_ref, idx_ref, gate_ref,
                  q_ref, expt_ref, gatet_ref, *, tm):
    h2 = _rms(x_ref[...], g_ref[...])
    h_ref[...] = h2
    q_ref[...] = jnp.dot(h2.astype(BF16), wq_ref[...], preferred_element_type=F32)

    def head(hh, carry):
        tops = []
        for p in range(2):
            c0 = pl.multiple_of(hh * (2 * PEER_DK) + p * PEER_DK, PEER_DK)
            qhp = q_ref[:, pl.ds(c0, PEER_DK)].astype(BF16)
            keys = keys_ref[p, hh]
            st = _qk(keys, qhp)
            tops.append(_topk_rows(st, K_SUB))
        (s0, i0), (s1, i1) = tops
        nb = lambda a: K_SUB if a == 0 else SUBLANES
        cand = jnp.concatenate([s0[a:a + 1] + s1[:nb(a)] for a in range(K_SUB)], axis=0)
        nc = K_SUB + (K_SUB - 1) * SUBLANES
        riota = lax.broadcasted_iota(I32, (nc, tm), 0)
        kiota = lax.broadcasted_iota(I32, (PEER_TOPK, tm), 0)
        best = jnp.zeros((PEER_TOPK, tm), F32)
        brow = jnp.zeros((PEER_TOPK, tm), I32)
        for i in range(PEER_TOPK):
            m = jnp.max(cand, axis=0, keepdims=True)
            am = jnp.min(jnp.where(cand == m, riota, nc), axis=0, keepdims=True)
            best = jnp.where(kiota == i, m, best)
            brow = jnp.where(kiota == i, am, brow)
            cand = jnp.where(riota == am, -jnp.inf, cand)
        tail = brow - K_SUB
        a_sel = jnp.where(tail < 0, 0, (tail >> 3) + 1)
        b_sel = jnp.where(tail < 0, brow, tail & (SUBLANES - 1))
        k0 = jnp.zeros((PEER_TOPK, tm), I32)
        k1 = jnp.zeros((PEER_TOPK, tm), I32)
        for a in range(K_SUB):
            k0 = jnp.where(a_sel == a, i0[a:a + 1], k0)
            k1 = jnp.where(b_sel == a, i1[a:a + 1], k1)
        bexp = k0 * N_KEYS + k1
        ex = jnp.exp(best - best[0:1])
        gate = ex / jnp.sum(ex, axis=0, keepdims=True)
        r0 = pl.multiple_of(hh * PEER_TOPK, PEER_TOPK)
        expt_ref[pl.ds(r0, PEER_TOPK), :] = bexp
        gatet_ref[pl.ds(r0, PEER_TOPK), :] = gate
        return carry

    lax.fori_loop(0, PEER_HEADS, head, 0)
    idx_ref[...] = expt_ref[...].T * HALF_TILE
    gate_ref[...] = gatet_ref[...].T


def _route(x2, g, wq, keys, tm):
    n = x2.shape[0]
    return pl.pallas_call(
        functools.partial(_route_kernel, tm=tm),
        grid=(n // tm,),
        in_specs=[
            pl.BlockSpec((tm, D_MODEL), lambda i: (i, 0)),
            pl.BlockSpec((1, D_MODEL), lambda i: (0, 0)),
            pl.BlockSpec((D_MODEL, 2 * PEER_DK * PEER_HEADS), lambda i: (0, 0)),
            pl.BlockSpec((2, PEER_HEADS, N_KEYS, PEER_DK), lambda i: (0, 0, 0, 0)),
        ],
        out_specs=[
            pl.BlockSpec((tm, D_MODEL), lambda i: (i, 0)),
            pl.BlockSpec((tm, PEER_PAIRS), lambda i: (i, 0)),
            pl.BlockSpec((tm, PEER_PAIRS), lambda i: (i, 0)),
        ],
        out_shape=[
            jax.ShapeDtypeStruct((n, D_MODEL), F32),
            jax.ShapeDtypeStruct((n, PEER_PAIRS), I32),
            jax.ShapeDtypeStruct((n, PEER_PAIRS), F32),
        ],
        scratch_shapes=[
            pltpu.VMEM((tm, 2 * PEER_DK * PEER_HEADS), F32),
            pltpu.VMEM((PEER_PAIRS, tm), I32),
            pltpu.VMEM((PEER_PAIRS, tm), F32),
        ],
        compiler_params=_cparams(("parallel",)),
        name="peer_route",
    )(x2, g, wq, keys)


RCOLS = PEER_PAIRS * SUBLANES
PEER_TB = 16


def _table_spec():
    return pl.BlockSpec((N_EXPERTS * HALF_TILE, LANES), lambda i: (0, 0),
                        pipeline_mode=pl.Buffered(1))


def _gather_rows(tab_ref, idx_ref, t):
    idx_t = idx_ref.at[t]
    tiles = [tab_ref[pl.ds(pl.multiple_of(idx_t[k], HALF_TILE), HALF_TILE), :]
             for k in range(PEER_PAIRS)]
    return pltpu.bitcast(jnp.concatenate(tiles, axis=0), BF16)


def _selector():
    s = lax.broadcasted_iota(I32, (SUBLANES, RCOLS), 0)
    c = lax.broadcasted_iota(I32, (SUBLANES, RCOLS), 1)
    return (c & (SUBLANES - 1)) == s


def _peer_u_kernel(idx_ref, tab_ref, h_ref, gate_ref, g_ref, w_ref, res_ref, *, tb):
    keep = _selector()

    for t in range(tb):
        r = _gather_rows(tab_ref, idx_ref, t)
        res = _qk(h_ref[t].astype(BF16), r)
        res_ref[t] = jnp.where(keep, res, 0.0)
    res = res_ref[...].reshape(tb * SUBLANES, RCOLS)
    res_hi = res.astype(BF16)
    res_lo = (res - res_hi.astype(F32)).astype(BF16)
    part = (jnp.dot(res_hi, g_ref[...], preferred_element_type=F32)
            + jnp.dot(res_lo, g_ref[...], preferred_element_type=F32))
    act = jnp.sum(part.reshape(tb, SUBLANES, PEER_PAIRS), axis=1)
    gelu = 0.5 * act * (1.0 + lax.erf(act * (2.0 ** -0.5)))
    w_ref[...] = gate_ref[...] * gelu


def _peer_u(idx, tab, h3, gate, g_mat, tb):
    n = idx.shape[0]
    return pl.pallas_call(
        functools.partial(_peer_u_kernel, tb=tb),
        grid=(n // tb,),
        in_specs=[
            pl.BlockSpec((tb, PEER_PAIRS), lambda i: (i, 0), memory_space=pltpu.SMEM),
            _table_spec(),
            pl.BlockSpec((tb, SUBLANES, LANES), lambda i: (i, 0, 0)),
            pl.BlockSpec((tb, PEER_PAIRS), lambda i: (i, 0)),
            pl.BlockSpec((RCOLS, PEER_PAIRS), lambda i: (0, 0)),
        ],
        out_specs=pl.BlockSpec((tb, PEER_PAIRS), lambda i: (i, 0)),
        out_shape=jax.ShapeDtypeStruct((n, PEER_PAIRS), F32),
        scratch_shapes=[pltpu.VMEM((tb, SUBLANES, RCOLS), F32)],
        compiler_params=_cparams(("arbitrary",)),
        name="peer_u",
    )(idx, tab, h3, gate, g_mat)


def _peer_v_kernel(idx_ref, tab_ref, w_ref, e_ref, o_ref, wrep_ref, *, tb):
    wrep_ref[...] = jnp.dot(w_ref[...].astype(BF16), e_ref[...], preferred_element_type=F32)
    keep = _selector()

    for t in range(tb):
        r = _gather_rows(tab_ref, idx_ref, t)
        sel = jnp.where(keep, wrep_ref[pl.ds(t, 1), :], 0.0)
        o_ref[t] = jnp.dot(sel.astype(BF16), r, preferred_element_type=F32)


def _peer_v(idx, tab, w, e_mat, tb):
    n = idx.shape[0]
    return pl.pallas_call(
        functools.partial(_peer_v_kernel, tb=tb),
        grid=(n // tb,),
        in_specs=[
            pl.BlockSpec((tb, PEER_PAIRS), lambda i: (i, 0), memory_space=pltpu.SMEM),
            _table_spec(),
            pl.BlockSpec((tb, PEER_PAIRS), lambda i: (i, 0)),
            pl.BlockSpec((PEER_PAIRS, RCOLS), lambda i: (0, 0)),
        ],
        out_specs=pl.BlockSpec((tb, SUBLANES, LANES), lambda i: (i, 0, 0)),
        out_shape=jax.ShapeDtypeStruct((n, SUBLANES, LANES), F32),
        scratch_shapes=[pltpu.VMEM((tb, RCOLS), F32)],
        compiler_params=_cparams(("arbitrary",)),
        name="peer_v",
    )(idx, tab, w, e_mat)


def _ple_kernel(x_ref, peer_ref, p_ref, g_ref, wg_ref, wp_ref, o_ref):
    x = x_ref[...] + peer_ref[...]
    gate = jax.nn.sigmoid(jnp.dot(_rms(x, g_ref[...]).astype(BF16), wg_ref[...],
                                  preferred_element_type=F32))
    emb = jnp.dot(p_ref[...].astype(BF16), wp_ref[...], preferred_element_type=F32)
    o_ref[...] = x + gate * emb


def _ple(x2, peer2, p2, g, wg, wp, tm):
    n = x2.shape[0]
    full = lambda shp: pl.BlockSpec(shp, lambda i: (0, 0))
    return pl.pallas_call(
        _ple_kernel,
        grid=(n // tm,),
        in_specs=[
            pl.BlockSpec((tm, D_MODEL), lambda i: (i, 0)),
            pl.BlockSpec((tm, D_MODEL), lambda i: (i, 0)),
            pl.BlockSpec((tm, PLE_DIM), lambda i: (i, 0)),
            full((1, D_MODEL)), full((D_MODEL, D_MODEL)), full((PLE_DIM, D_MODEL)),
        ],
        out_specs=pl.BlockSpec((tm, D_MODEL), lambda i: (i, 0)),
        out_shape=jax.ShapeDtypeStruct((n, D_MODEL), F32),
        compiler_params=_cparams(("parallel",)),
        name="ple",
    )(x2, peer2, p2, g, wg, wp)


def _relayout_w_in(w_in):
    o = np.cumsum([0, 512, 512, 512, Q_LORA, KV_LORA, MLA_ROPE, 512, 512, 512, 3 * D_MODEL])
    seg = lambda k: w_in[:, o[k]:o[k + 1]]
    pad = jnp.zeros((D_MODEL, 512 - Q_LORA - KV_LORA - MLA_ROPE), w_in.dtype)
    return jnp.concatenate([seg(9), seg(0), seg(1), seg(2), seg(6), seg(7), seg(8),
                            seg(3), seg(4), seg(5), pad], axis=1).astype(BF16)


def _pad_heads(w, width):
    k = w.shape[0]
    w3 = w.reshape(k, MLA_HEADS, width)
    return jnp.pad(w3, ((0, 0), (0, 0), (0, LANES - width))).reshape(k, MLA_HEADS * LANES)


def _pack_table(tab):
    bits = lax.bitcast_convert_type(tab.astype(BF16), jnp.uint16).astype(jnp.uint32)
    bits = bits.reshape(N_EXPERTS, HALF_TILE, 2, LANES)
    return (bits[:, :, 0] | (bits[:, :, 1] << 16)).reshape(N_EXPERTS * HALF_TILE, LANES)


def _group_matrices():
    e = (np.arange(RCOLS)[None, :] // SUBLANES == np.arange(PEER_PAIRS)[:, None])
    return jnp.asarray(e, BF16), jnp.asarray(e.T, BF16)


def _rope_tables(seq):
    half = MLA_ROPE // 2
    freqs = ROPE_THETA ** (-np.arange(half, dtype=np.float32) / half)
    ang = np.arange(seq, dtype=np.float32)[:, None] * freqs[None, :]
    cos = np.ones((seq, LANES), np.float32)
    sin = np.zeros((seq, LANES), np.float32)
    cos[:, MLA_NOPE:MLA_NOPE + half] = np.cos(ang)
    cos[:, MLA_NOPE + half:MLA_QK] = np.cos(ang)
    sin[:, MLA_NOPE:MLA_NOPE + half] = -np.sin(ang)
    sin[:, MLA_NOPE + half:MLA_QK] = np.sin(ang)
    return jnp.asarray(cos), jnp.asarray(sin)


def _tile(n, pref):
    return pref if n % pref == 0 else n


def _layer(x2, p2, layer_idx, batch, seq, g_mix, w_in, g_diff_q, g_diff_k, lam_q1, lam_k1, lam_q2,
           lam_k2, g_diff_sub, g_cq, w_uq, g_ckv, w_ukv, g_mla_q, g_mla_k, conv_w, w_br_diff,
           w_br_mla, w_br_conv, w_out, g_ffn, w_query, sub_keys, u_experts, v_experts, g_ple,
           w_ple_gate, w_ple_proj):
    n = x2.shape[0]
    lambda_init = 0.8 - 0.6 * math.exp(-0.3 * layer_idx)
    tm = _tile(seq, 256)
    tq = _tile(seq, 512)
    row = lambda v: v.reshape(1, -1)

    z2 = _inproj(x2, row(g_mix), _relayout_w_in(w_in), tm)
    z3 = z2.reshape(batch, seq, Z_W)

    slopes = jnp.asarray(2.0 ** (-8.0 * np.arange(1, DIFF_HEADS + 1, dtype=np.float32) / DIFF_HEADS))
    gq2 = jnp.tile(g_diff_q, 2).reshape(1, LANES)
    gk2 = jnp.tile(g_diff_k, 2).reshape(1, LANES)
    lam4 = jnp.stack([lam_q1, lam_k1, lam_q2, lam_k2])
    yd = _diff_attention(z3, slopes, gq2, gk2, lam4, row(g_diff_sub), tq, lambda_init)

    w_ukv3 = w_ukv.reshape(KV_LORA, MLA_HEADS, MLA_NOPE + MLA_V)
    wuk_p = _pad_heads(w_ukv3[:, :, :MLA_NOPE].reshape(KV_LORA, -1), MLA_NOPE).astype(BF16)
    wuv_p = w_ukv3[:, :, MLA_NOPE:].reshape(KV_LORA, -1).astype(BF16)
    wuq_p = _pad_heads(w_uq, MLA_QK).astype(BF16)
    gq_p = jnp.pad(g_mla_q, (0, LANES - MLA_QK)).reshape(1, LANES)
    gk_p = jnp.pad(g_mla_k, (0, LANES - MLA_QK)).reshape(1, LANES)
    cos_t, sin_t = _rope_tables(seq)
    qm, km, vm = _mla_proj(z2, row(g_cq), wuq_p, row(g_ckv), wuk_p, wuv_p, gq_p, gk_p,
                           cos_t, sin_t, tm, seq)
    ym = _mla_attention(qm.reshape(batch, seq, -1), km.reshape(batch, seq, -1),
                        vm.reshape(batch, seq, -1), tq)

    x2 = _merge(x2, yd.reshape(n, DIFF_W), ym.reshape(n, -1), z2, conv_w,
                w_br_diff.astype(BF16), w_br_mla.astype(BF16), w_br_conv.astype(BF16),
                w_out.astype(BF16), tm, seq)

    h2, idx, gate = _route(x2, row(g_ffn), w_query.astype(BF16), sub_keys.astype(BF16),
                           _tile(seq, ROUTE_TM))
    tb = _tile(n, PEER_TB)
    e_mat, g_mat = _group_matrices()
    w = _peer_u(idx, _pack_table(u_experts), h2.reshape(n, SUBLANES, LANES), gate, g_mat, tb)
    peer = _peer_v(idx, _pack_table(v_experts), w, e_mat, tb)

    return _ple(x2, peer.reshape(n, D_MODEL), p2, row(g_ple), w_ple_gate.astype(BF16),
                w_ple_proj.astype(BF16), tm)


def kernel(x, p, g_mix, w_in, g_diff_q, g_diff_k, lam_q1, lam_k1, lam_q2, lam_k2, g_diff_sub, g_cq,
           w_uq, g_ckv, w_ukv, g_mla_q, g_mla_k, conv_w, w_br_diff, w_br_mla, w_br_conv, w_out,
           g_ffn, w_query, sub_keys, u_experts, v_experts, g_ple, w_ple_gate, w_ple_proj):
    batch, seq, _ = x.shape
    n = batch * seq
    x2 = x.reshape(n, D_MODEL)
    per_layer = (g_mix, w_in, g_diff_q, g_diff_k, lam_q1, lam_k1, lam_q2, lam_k2, g_diff_sub, g_cq,
                 w_uq, g_ckv, w_ukv, g_mla_q, g_mla_k, conv_w, w_br_diff, w_br_mla, w_br_conv,
                 w_out, g_ffn, w_query, sub_keys, u_experts, v_experts, g_ple, w_ple_gate,
                 w_ple_proj)
    for i in range(p.shape[0]):
        x2 = _layer(x2, p[i].reshape(n, PLE_DIM), i, batch, seq, *(a[i] for a in per_layer))
    return x2.reshape(batch, seq, D_MODEL)
```

```python
import functools
import math

import jax
import jax.numpy as jnp
import numpy as np
from jax import lax
from jax.experimental import pallas as pl
from jax.experimental.pallas import tpu as pltpu

F32 = jnp.float32
BF16 = jnp.bfloat16
I32 = jnp.int32

D_MODEL = 1024
DIFF_HEADS = 4
DIFF_HD = 64
DIFF_W = 512
MLA_HEADS = 8
MLA_NOPE = 64
MLA_ROPE = 32
MLA_V = 64
MLA_QK = 96
Q_LORA = 256
KV_LORA = 128
ROPE_THETA = 10000.0
CONV_W = 512
CONV_K = 3
PEER_HEADS = 8
N_KEYS = 128
N_EXPERTS = N_KEYS * N_KEYS
PEER_DK = 128
K_SUB = 16
PEER_TOPK = 16
PEER_PAIRS = PEER_HEADS * PEER_TOPK
PLE_DIM = 256
NORM_EPS = 1e-6
NEG_INF = -1e30

LANES = 128
SUBLANES = 8
HALF_TILE = SUBLANES // 2
VMEM_LIMIT = 56 * 1024 * 1024

Z_GATES = 0
Z_DQ = 3072
Z_DK = 3584
Z_DV = 4096
Z_CB = 4608
Z_CC = 5120
Z_CX = 5632
Z_MLA = 6144
Z_W = 6656


def _cparams(sem, vmem=VMEM_LIMIT):
    return pltpu.CompilerParams(dimension_semantics=sem, vmem_limit_bytes=vmem)


def _rms(x, g):
    return x * lax.rsqrt(jnp.mean(x * x, axis=-1, keepdims=True) + NORM_EPS) * g


def _inproj_kernel(x_ref, g_ref, w_ref, z_ref):
    h = _rms(x_ref[...], g_ref[...]).astype(BF16)
    z_ref[...] = jnp.dot(h, w_ref[...], preferred_element_type=F32)


def _inproj(x2, g, w_in_r, tm):
    n = x2.shape[0]
    return pl.pallas_call(
        _inproj_kernel,
        grid=(n // tm,),
        in_specs=[
            pl.BlockSpec((tm, D_MODEL), lambda i: (i, 0)),
            pl.BlockSpec((1, D_MODEL), lambda i: (0, 0)),
            pl.BlockSpec((D_MODEL, Z_W), lambda i: (0, 0)),
        ],
        out_specs=pl.BlockSpec((tm, Z_W), lambda i: (i, 0)),
        out_shape=jax.ShapeDtypeStruct((n, Z_W), F32),
        compiler_params=_cparams(("parallel",)),
        name="inproj",
    )(x2, g, w_in_r)


def _qk(q, kt):
    return lax.dot_general(q, kt, (((1,), (1,)), ((), ())), preferred_element_type=F32)


def _flash_update(s, m_ref, l_ref, acc_ref, vt, diag):
    tq, tk = s.shape
    if diag:
        row = lax.broadcasted_iota(I32, s.shape, 0)
        col = lax.broadcasted_iota(I32, s.shape, 1)
        s = jnp.where(row >= col, s, NEG_INF)
    m_old = m_ref[...]
    m_new = jnp.maximum(m_old, jnp.max(s, axis=-1, keepdims=True))
    p = jnp.exp(s - jnp.tile(m_new, (1, tk // LANES)))
    alpha = jnp.exp(m_old - m_new)
    l_ref[...] = alpha * l_ref[...] + jnp.sum(p, axis=-1, keepdims=True)
    m_ref[...] = m_new
    acc_ref[...] = alpha * acc_ref[...] + jnp.dot(p.astype(BF16), vt, preferred_element_type=F32)


def _flash_init(m_ref, l_ref, acc_ref):
    m_ref[...] = jnp.full(m_ref.shape, NEG_INF, F32)
    l_ref[...] = jnp.zeros(l_ref.shape, F32)
    acc_ref[...] = jnp.zeros(acc_ref.shape, F32)


def _flash_scratch(tq):
    return [pltpu.VMEM((tq, LANES), F32)] * 3


ALIBI_BITS = 6
ALIBI_SPLIT = 1 << ALIBI_BITS


def _diff_attn_kernel(slope_ref, q_ref, k_ref, v_ref, gq_ref, gk_ref, lam_ref, gsub_ref, o_ref,
                      k1_ref, k2_ref, m1_ref, l1_ref, a1_ref, m2_ref, l2_ref, a2_ref,
                      *, tq, seq, lambda_init):
    h = pl.program_id(1)
    qi = pl.program_id(2)
    lane = lax.broadcasted_iota(I32, (1, LANES), 1)
    first = lane < DIFF_HD
    slope = slope_ref[h]

    def halfnorm(x, g):
        x2 = x * x
        s_all = jnp.sum(x2, axis=-1, keepdims=True)
        s_lo = jnp.sum(jnp.where(first, x2, 0.0), axis=-1, keepdims=True)
        ms = jnp.where(first, s_lo, s_all - s_lo) * (1.0 / DIFF_HD)
        return x * lax.rsqrt(ms + NORM_EPS) * g

    def alibi_cols(pos0, base, query):
        pos = lax.broadcasted_iota(I32, (tq, 1), 0) + pos0
        hi = lax.shift_right_logical(pos, ALIBI_BITS).astype(F32) * (slope * ALIBI_SPLIT)
        lo = (pos & (ALIBI_SPLIT - 1)).astype(F32) * slope
        if query:
            c0, c1, c2, c3 = -hi, -lo, 1.0, 1.0
        else:
            c0, c1, c2, c3 = 1.0, 1.0, hi, lo
        return jnp.where(lane == base, c0, jnp.where(lane == base + 1, c1, jnp.where(
            lane == base + 2, c2, jnp.where(lane == base + 3, c3, 0.0))))

    @pl.when(qi == 0)
    def _():
        def body(c, carry):
            r0 = pl.multiple_of(c * tq, tq)
            kn = halfnorm(k_ref[pl.ds(r0, tq), :], gk_ref[...])
            k1_ref[pl.ds(r0, tq), :] = jnp.where(first, kn, alibi_cols(r0, DIFF_HD, False)).astype(BF16)
            k2_ref[pl.ds(r0, tq), :] = jnp.where(first, alibi_cols(r0, 0, False), kn).astype(BF16)
            return carry
        lax.fori_loop(0, seq // tq, body, 0)

    q = halfnorm(q_ref[...], gq_ref[...]) * (DIFF_HD ** -0.5)
    q1 = jnp.where(first, q, alibi_cols(qi * tq, DIFF_HD, True)).astype(BF16)
    q2 = jnp.where(first, alibi_cols(qi * tq, 0, True), q).astype(BF16)
    _flash_init(m1_ref, l1_ref, a1_ref)
    _flash_init(m2_ref, l2_ref, a2_ref)

    def step(j, diag):
        rows = pl.ds(pl.multiple_of(j * tq, tq), tq)
        vt = v_ref[rows, :].astype(BF16)
        _flash_update(_qk(q1, k1_ref[rows, :]), m1_ref, l1_ref, a1_ref, vt, diag)
        _flash_update(_qk(q2, k2_ref[rows, :]), m2_ref, l2_ref, a2_ref, vt, diag)

    def off_diag(j, carry):
        step(j, False)
        return carry

    lax.fori_loop(0, qi, off_diag, 0)
    step(qi, True)

    lam = lam_ref[...]
    e1 = jnp.exp(jnp.sum(lam[0:1] * lam[1:2], axis=-1, keepdims=True))
    e2 = jnp.exp(jnp.sum(lam[2:3] * lam[3:4], axis=-1, keepdims=True))
    lam_full = e1 - e2 + lambda_init
    o = a1_ref[...] / l1_ref[...] - lam_full * (a2_ref[...] / l2_ref[...])
    o_ref[...] = _rms(o, gsub_ref[...]) * (1.0 - lambda_init)


def _diff_attention(z3, slopes, gq2, gk2, lam4, gsub, tq, lambda_init):
    b, s, _ = z3.shape
    kern = functools.partial(_diff_attn_kernel, tq=tq, seq=s, lambda_init=lambda_init)
    return pl.pallas_call(
        kern,
        grid=(b, DIFF_HEADS, s // tq),
        in_specs=[
            pl.BlockSpec(memory_space=pltpu.SMEM),
            pl.BlockSpec((None, tq, LANES), lambda bi, h, qi: (bi, qi, Z_DQ // LANES + h)),
            pl.BlockSpec((None, s, LANES), lambda bi, h, qi: (bi, 0, Z_DK // LANES + h)),
            pl.BlockSpec((None, s, LANES), lambda bi, h, qi: (bi, 0, Z_DV // LANES + h)),
            pl.BlockSpec((1, LANES), lambda bi, h, qi: (0, 0)),
            pl.BlockSpec((1, LANES), lambda bi, h, qi: (0, 0)),
            pl.BlockSpec((4, DIFF_HD), lambda bi, h, qi: (0, 0)),
            pl.BlockSpec((1, LANES), lambda bi, h, qi: (0, 0)),
        ],
        out_specs=pl.BlockSpec((None, tq, LANES), lambda bi, h, qi: (bi, qi, h)),
        out_shape=jax.ShapeDtypeStruct((b, s, DIFF_W), F32),
        scratch_shapes=[pltpu.VMEM((s, LANES), BF16), pltpu.VMEM((s, LANES), BF16)]
        + _flash_scratch(tq) + _flash_scratch(tq),
        compiler_params=_cparams(("parallel", "parallel", "arbitrary")),
        name="diff_attn",
    )(slopes, z3, z3, z3, gq2, gk2, lam4, gsub)


def _mla_proj_kernel(z_ref, gcq_ref, wuq_ref, gckv_ref, wuk_ref, wuv_ref, gq_ref, gk_ref,
                     cos_ref, sin_ref, qm_ref, km_ref, vm_ref):
    z = z_ref[...]
    cq = z[:, :Q_LORA]
    ckv = z[:, Q_LORA:Q_LORA + KV_LORA]
    krb = z[:, Q_LORA + KV_LORA:]
    cqn = _rms(cq, gcq_ref[...]).astype(BF16)
    ckvn = _rms(ckv, gckv_ref[...]).astype(BF16)
    q = jnp.dot(cqn, wuq_ref[...], preferred_element_type=F32)
    kn = jnp.dot(ckvn, wuk_ref[...], preferred_element_type=F32)
    v = jnp.dot(ckvn, wuv_ref[...], preferred_element_type=F32)
    vm_ref[...] = v.astype(BF16)

    lane = lax.broadcasted_iota(I32, (1, LANES), 1)
    in_rope = (lane >= MLA_NOPE) & (lane < MLA_QK)
    kr = jnp.where(in_rope, pltpu.roll(krb, MLA_NOPE, 1), 0.0)
    cos = cos_ref[...]
    sin = sin_ref[...]
    gq = gq_ref[...]
    gk = gk_ref[...]
    half = MLA_ROPE // 2
    src = lax.broadcasted_iota(I32, (LANES, LANES), 0)
    dst = lax.broadcasted_iota(I32, (LANES, LANES), 1)
    x1_dst = (dst >= MLA_NOPE) & (dst < MLA_NOPE + half)
    x2_dst = (dst >= MLA_NOPE + half) & (dst < MLA_QK)
    swap_m = jnp.where((x1_dst & (src == dst + half)) | (x2_dst & (src == dst - half)),
                       1.0, 0.0).astype(BF16)
    ones_m = jnp.ones((LANES, LANES), BF16)

    def mm_exactish(x, m):
        hi = x.astype(BF16)
        lo = (x - hi.astype(F32)).astype(BF16)
        return (jnp.dot(hi, m, preferred_element_type=F32)
                + jnp.dot(lo, m, preferred_element_type=F32))

    def norm_rope(xh, g):
        ms = mm_exactish(xh * xh, ones_m) * (1.0 / MLA_QK)
        xn = xh * lax.rsqrt(ms + NORM_EPS) * g
        return xn * cos + mm_exactish(xn, swap_m) * sin

    for hh in range(MLA_HEADS):
        sl = slice(hh * LANES, (hh + 1) * LANES)
        qm_ref[:, sl] = (norm_rope(q[:, sl], gq) * (MLA_QK ** -0.5)).astype(BF16)
        km_ref[:, sl] = norm_rope(kn[:, sl] + kr, gk).astype(BF16)


def _mla_proj(z2, gcq, wuq_p, gckv, wuk_p, wuv_p, gq_p, gk_p, cos_t, sin_t, tm, seq):
    n = z2.shape[0]
    nsb = seq // tm
    full = lambda shp: pl.BlockSpec(shp, lambda i: (0, 0))
    return pl.pallas_call(
        _mla_proj_kernel,
        grid=(n // tm,),
        in_specs=[
            pl.BlockSpec((tm, 512), lambda i: (i, Z_MLA // 512)),
            full((1, Q_LORA)), full((Q_LORA, MLA_HEADS * LANES)),
            full((1, KV_LORA)), full((KV_LORA, MLA_HEADS * LANES)), full((KV_LORA, MLA_HEADS * MLA_V)),
            full((1, LANES)), full((1, LANES)),
            pl.BlockSpec((tm, LANES), lambda i: (i % nsb, 0)),
            pl.BlockSpec((tm, LANES), lambda i: (i % nsb, 0)),
        ],
        out_specs=[
            pl.BlockSpec((tm, MLA_HEADS * LANES), lambda i: (i, 0)),
            pl.BlockSpec((tm, MLA_HEADS * LANES), lambda i: (i, 0)),
            pl.BlockSpec((tm, MLA_HEADS * MLA_V), lambda i: (i, 0)),
        ],
        out_shape=[
            jax.ShapeDtypeStruct((n, MLA_HEADS * LANES), BF16),
            jax.ShapeDtypeStruct((n, MLA_HEADS * LANES), BF16),
            jax.ShapeDtypeStruct((n, MLA_HEADS * MLA_V), BF16),
        ],
        compiler_params=_cparams(("parallel",)),
        name="mla_proj",
    )(z2, gcq, wuq_p, gckv, wuk_p, wuv_p, gq_p, gk_p, cos_t, sin_t)


def _mla_attn_kernel(q_ref, k_ref, v_ref, o_ref, ma_ref, la_ref, aa_ref, mb_ref, lb_ref, ab_ref,
                     *, tq):
    qi = pl.program_id(2)
    _flash_init(ma_ref, la_ref, aa_ref)
    _flash_init(mb_ref, lb_ref, ab_ref)

    def step(j, diag):
        rows = pl.ds(pl.multiple_of(j * tq, tq), tq)
        vt = v_ref[rows, :]
        _flash_update(_qk(q_ref[:, :LANES], k_ref[rows, :LANES]), ma_ref, la_ref, aa_ref, vt, diag)
        _flash_update(_qk(q_ref[:, LANES:], k_ref[rows, LANES:]), mb_ref, lb_ref, ab_ref, vt, diag)

    def off_diag(j, carry):
        step(j, False)
        return carry

    lax.fori_loop(0, qi, off_diag, 0)
    step(qi, True)
    lane = lax.broadcasted_iota(I32, (1, LANES), 1)
    o_ref[...] = jnp.where(lane < MLA_V, aa_ref[...] / la_ref[...], ab_ref[...] / lb_ref[...])


def _mla_attention(qm3, km3, vm3, tq):
    b, s, _ = qm3.shape
    return pl.pallas_call(
        functools.partial(_mla_attn_kernel, tq=tq),
        grid=(b, MLA_HEADS // 2, s // tq),
        in_specs=[
            pl.BlockSpec((None, tq, 2 * LANES), lambda bi, hp, qi: (bi, qi, hp)),
            pl.BlockSpec((None, s, 2 * LANES), lambda bi, hp, qi: (bi, 0, hp)),
            pl.BlockSpec((None, s, LANES), lambda bi, hp, qi: (bi, 0, hp)),
        ],
        out_specs=pl.BlockSpec((None, tq, LANES), lambda bi, hp, qi: (bi, qi, hp)),
        out_shape=jax.ShapeDtypeStruct((b, s, MLA_HEADS * MLA_V), F32),
        scratch_shapes=_flash_scratch(tq) + _flash_scratch(tq),
        compiler_params=_cparams(("parallel", "parallel", "arbitrary")),
        name="mla_attn",
    )(qm3, km3, vm3)


def _merge_kernel(x_ref, yd_ref, ym_ref, cb_ref, cc_ref, cx_ref, cch_ref, cxh_ref,
                  g0_ref, g1_ref, g2_ref, cw_ref, wd_ref, wm_ref, wc_ref, wo_ref,
                  o_ref, ucat_ref, *, tm, seq):
    i = pl.program_id(0)
    at_seq_start = (i * tm) % seq == 0
    halo = cch_ref[...] * cxh_ref[...]
    ucat_ref[0:SUBLANES, :] = jnp.where(at_seq_start, 0.0, halo)
    ucat_ref[SUBLANES:, :] = cc_ref[...] * cx_ref[...]
    cw = cw_ref[...]
    conv = (cw[0:1] * ucat_ref[SUBLANES - 2:SUBLANES - 2 + tm, :]
            + cw[1:2] * ucat_ref[SUBLANES - 1:SUBLANES - 1 + tm, :]
            + cw[2:3] * ucat_ref[SUBLANES:, :])
    yc = cb_ref[...] * conv

    def br(y, w_ref, g_ref):
        return jax.nn.sigmoid(g_ref[...]) * jnp.dot(y.astype(BF16), w_ref[...],
                                                    preferred_element_type=F32)

    mixed = br(yd_ref[...], wd_ref, g0_ref) + br(ym_ref[...], wm_ref, g1_ref) + br(yc, wc_ref, g2_ref)
    o_ref[...] = x_ref[...] + jnp.dot(mixed.astype(BF16), wo_ref[...], preferred_element_type=F32)


def _merge(x2, yd2, ym2, z2, conv_w, wd, wm, wc, wo, tm, seq):
    n = x2.shape[0]
    hb = tm // SUBLANES
    full = lambda shp: pl.BlockSpec(shp, lambda i: (0, 0))
    zc = lambda off: pl.BlockSpec((tm, 512), lambda i: (i, off // 512))
    zh = lambda off: pl.BlockSpec((SUBLANES, 512), lambda i: (jnp.maximum(i * hb - 1, 0), off // 512))
    zg = lambda k: pl.BlockSpec((tm, D_MODEL), lambda i: (i, k))
    return pl.pallas_call(
        functools.partial(_merge_kernel, tm=tm, seq=seq),
        grid=(n // tm,),
        in_specs=[
            pl.BlockSpec((tm, D_MODEL), lambda i: (i, 0)),
            pl.BlockSpec((tm, DIFF_W), lambda i: (i, 0)),
            pl.BlockSpec((tm, 512), lambda i: (i, 0)),
            zc(Z_CB), zc(Z_CC), zc(Z_CX), zh(Z_CC), zh(Z_CX),
            zg(0), zg(1), zg(2),
            full((CONV_K, CONV_W)),
            full((DIFF_W, D_MODEL)), full((512, D_MODEL)), full((CONV_W, D_MODEL)),
            full((D_MODEL, D_MODEL)),
        ],
        out_specs=pl.BlockSpec((tm, D_MODEL), lambda i: (i, 0)),
        out_shape=jax.ShapeDtypeStruct((n, D_MODEL), F32),
        scratch_shapes=[pltpu.VMEM((tm + SUBLANES, CONV_W), F32)],
        compiler_params=_cparams(("parallel",)),
        name="merge",
    )(x2, yd2, ym2, z2, z2, z2, z2, z2, z2, z2, z2, conv_w, wd, wm, wc, wo)


ROUTE_TM = 512


def _topk_rows(s, k):
    rows, t = s.shape
    riota = lax.broadcasted_iota(I32, (rows, t), 0)
    kiota = lax.broadcasted_iota(I32, (k, t), 0)
    vals = jnp.zeros((k, t), F32)
    idxs = jnp.zeros((k, t), I32)
    for i in range(k):
        m = jnp.max(s, axis=0, keepdims=True)
        am = jnp.min(jnp.where(s == m, riota, rows), axis=0, keepdims=True)
        vals = jnp.where(kiota == i, m, vals)
        idxs = jnp.where(kiota == i, am, idxs)
        s = jnp.where(riota == am, -jnp.inf, s)
    return vals, idxs


def _route_kernel(x_ref, g_ref, wq_ref, keys_ref, h_ref, idx_ref, gate_ref,
                  q_ref, expt_ref, gatet_ref, *, tm):
    h2 = _rms(x_ref[...], g_ref[...])
    h_ref[...] = h2
    q_ref[...] = jnp.dot(h2.astype(BF16), wq_ref[...], preferred_element_type=F32)

    def head(hh, carry):
        tops = []
        for p in range(2):
            c0 = pl.multiple_of(hh * (2 * PEER_DK) + p * PEER_DK, PEER_DK)
            qhp = q_ref[:, pl.ds(c0, PEER_DK)].astype(BF16)
            keys = keys_ref[p, hh]
            st = _qk(keys, qhp)
            tops.append(_topk_rows(st, K_SUB))
        (s0, i0), (s1, i1) = tops
        nb = lambda a: K_SUB if a == 0 else SUBLANES
        cand = jnp.concatenate([s0[a:a + 1] + s1[:nb(a)] for a in range(K_SUB)], axis=0)
        nc = K_SUB + (K_SUB - 1) * SUBLANES
        riota = lax.broadcasted_iota(I32, (nc, tm), 0)
        kiota = lax.broadcasted_iota(I32, (PEER_TOPK, tm), 0)
        best = jnp.zeros((PEER_TOPK, tm), F32)
        brow = jnp.zeros((PEER_TOPK, tm), I32)
        for i in range(PEER_TOPK):
            m = jnp.max(cand, axis=0, keepdims=True)
            am = jnp.min(jnp.where(cand == m, riota, nc), axis=0, keepdims=True)
            best = jnp.where(kiota == i, m, best)
            brow = jnp.where(kiota == i, am, brow)
            cand = jnp.where(riota == am, -jnp.inf, cand)
        tail = brow - K_SUB
        a_sel = jnp.where(tail < 0, 0, (tail >> 3) + 1)
        b_sel = jnp.where(tail < 0, brow, tail & (SUBLANES - 1))
        k0 = jnp.zeros((PEER_TOPK, tm), I32)
        k1 = jnp.zeros((PEER_TOPK, tm), I32)
        for a in range(K_SUB):
            k0 = jnp.where(a_sel == a, i0[a:a + 1], k0)
            k1 = jnp.where(b_sel == a, i1[a:a + 1], k1)
        bexp = k0 * N_KEYS + k1
        ex = jnp.exp(best - best[0:1])
        gate = ex / jnp.sum(ex, axis=0, keepdims=True)
        r0 = pl.multiple_of(hh * PEER_TOPK, PEER_TOPK)
        expt_ref[pl.ds(r0, PEER_TOPK), :] = bexp
        gatet_ref[pl.ds(r0, PEER_TOPK), :] = gate
        return carry

    lax.fori_loop(0, PEER_HEADS, head, 0)
    idx_ref[...] = expt_ref[...].T * HALF_TILE
    gate_ref[...] = gatet_ref[...].T


def _route(x2, g, wq, keys, tm):
    n = x2.shape[0]
    return pl.pallas_call(
        functools.partial(_route_kernel, tm=tm),
        grid=(n // tm,),
        in_specs=[
            pl.BlockSpec((tm, D_MODEL), lambda i: (i, 0)),
            pl.BlockSpec((1, D_MODEL), lambda i: (0, 0)),
            pl.BlockSpec((D_MODEL, 2 * PEER_DK * PEER_HEADS), lambda i: (0, 0)),
            pl.BlockSpec((2, PEER_HEADS, N_KEYS, PEER_DK), lambda i: (0, 0, 0, 0)),
        ],
        out_specs=[
            pl.BlockSpec((tm, D_MODEL), lambda i: (i, 0)),
            pl.BlockSpec((tm, PEER_PAIRS), lambda i: (i, 0)),
            pl.BlockSpec((tm, PEER_PAIRS), lambda i: (i, 0)),
        ],
        out_shape=[
            jax.ShapeDtypeStruct((n, D_MODEL), F32),
            jax.ShapeDtypeStruct((n, PEER_PAIRS), I32),
            jax.ShapeDtypeStruct((n, PEER_PAIRS), F32),
        ],
        scratch_shapes=[
            pltpu.VMEM((tm, 2 * PEER_DK * PEER_HEADS), F32),
            pltpu.VMEM((PEER_PAIRS, tm), I32),
            pltpu.VMEM((PEER_PAIRS, tm), F32),
        ],
        compiler_params=_cparams(("parallel",)),
        name="peer_route",
    )(x2, g, wq, keys)


RCOLS = PEER_PAIRS * SUBLANES
PEER_TB = 16


def _table_spec():
    return pl.BlockSpec((N_EXPERTS * HALF_TILE, LANES), lambda i: (0, 0),
                        pipeline_mode=pl.Buffered(1))


def _gather_rows(tab_ref, idx_ref, t):
    idx_t = idx_ref.at[t]
    tiles = [tab_ref[pl.ds(pl.multiple_of(idx_t[k], HALF_TILE), HALF_TILE), :]
             for k in range(PEER_PAIRS)]
    return pltpu.bitcast(jnp.concatenate(tiles, axis=0), BF16)


def _selector():
    s = lax.broadcasted_iota(I32, (SUBLANES, RCOLS), 0)
    c = lax.broadcasted_iota(I32, (SUBLANES, RCOLS), 1)
    return (c & (SUBLANES - 1)) == s


def _peer_u_kernel(idx_ref, tab_ref, h_ref, gate_ref, g_ref, w_ref, res_ref, *, tb):
    keep = _selector()

    for t in range(tb):
        r = _gather_rows(tab_ref, idx_ref, t)
        res = _qk(h_ref[t].astype(BF16), r)
        res_ref[t] = jnp.where(keep, res, 0.0)
    res = res_ref[...].reshape(tb * SUBLANES, RCOLS)
    res_hi = res.astype(BF16)
    res_lo = (res - res_hi.astype(F32)).astype(BF16)
    part = (jnp.dot(res_hi, g_ref[...], preferred_element_type=F32)
            + jnp.dot(res_lo, g_ref[...], preferred_element_type=F32))
    act = jnp.sum(part.reshape(tb, SUBLANES, PEER_PAIRS), axis=1)
    gelu = 0.5 * act * (1.0 + lax.erf(act * (2.0 ** -0.5)))
    w_ref[...] = gate_ref[...] * gelu


def _peer_u(idx, tab, h3, gate, g_mat, tb):
    n = idx.shape[0]
    return pl.pallas_call(
        functools.partial(_peer_u_kernel, tb=tb),
        grid=(n // tb,),
        in_specs=[
            pl.BlockSpec((tb, PEER_PAIRS), lambda i: (i, 0), memory_space=pltpu.SMEM),
            _table_spec(),
            pl.BlockSpec((tb, SUBLANES, LANES), lambda i: (i, 0, 0)),
            pl.BlockSpec((tb, PEER_PAIRS), lambda i: (i, 0)),
            pl.BlockSpec((RCOLS, PEER_PAIRS), lambda i: (0, 0)),
        ],
        out_specs=pl.BlockSpec((tb, PEER_PAIRS), lambda i: (i, 0)),
        out_shape=jax.ShapeDtypeStruct((n, PEER_PAIRS), F32),
        scratch_shapes=[pltpu.VMEM((tb, SUBLANES, RCOLS), F32)],
        compiler_params=_cparams(("arbitrary",)),
        name="peer_u",
    )(idx, tab, h3, gate, g_mat)


def _peer_v_kernel(idx_ref, tab_ref, w_ref, e_ref, o_ref, wrep_ref, *, tb):
    wrep_ref[...] = jnp.dot(w_ref[...].astype(BF16), e_ref[...], preferred_element_type=F32)
    keep = _selector()

    for t in range(tb):
        r = _gather_rows(tab_ref, idx_ref, t)
        sel = jnp.where(keep, wrep_ref[pl.ds(t, 1), :], 0.0)
        o_ref[t] = jnp.dot(sel.astype(BF16), r, preferred_element_type=F32)


def _peer_v(idx, tab, w, e_mat, tb):
    n = idx.shape[0]
    return pl.pallas_call(
        functools.partial(_peer_v_kernel, tb=tb),
        grid=(n // tb,),
        in_specs=[
            pl.BlockSpec((tb, PEER_PAIRS), lambda i: (i, 0), memory_space=pltpu.SMEM),
            _table_spec(),
            pl.BlockSpec((tb, PEER_PAIRS), lambda i: (i, 0)),
            pl.BlockSpec((PEER_PAIRS, RCOLS), lambda i: (0, 0)),
        ],
        out_specs=pl.BlockSpec((tb, SUBLANES, LANES), lambda i: (i, 0, 0)),
        out_shape=jax.ShapeDtypeStruct((n, SUBLANES, LANES), F32),
        scratch_shapes=[pltpu.VMEM((tb, RCOLS), F32)],
        compiler_params=_cparams(("arbitrary",)),
        name="peer_v",
    )(idx, tab, w, e_mat)


def _ple_kernel(x_ref, peer_ref, p_ref, g_ref, wg_ref, wp_ref, o_ref):
    x = x_ref[...] + peer_ref[...]
    gate = jax.nn.sigmoid(jnp.dot(_rms(x, g_ref[...]).astype(BF16), wg_ref[...],
                                  preferred_element_type=F32))
    emb = jnp.dot(p_ref[...].astype(BF16), wp_ref[...], preferred_element_type=F32)
    o_ref[...] = x + gate * emb


def _ple(x2, peer2, p2, g, wg, wp, tm):
    n = x2.shape[0]
    full = lambda shp: pl.BlockSpec(shp, lambda i: (0, 0))
    return pl.pallas_call(
        _ple_kernel,
        grid=(n // tm,),
        in_specs=[
            pl.BlockSpec((tm, D_MODEL), lambda i: (i, 0)),
            pl.BlockSpec((tm, D_MODEL), lambda i: (i, 0)),
            pl.BlockSpec((tm, PLE_DIM), lambda i: (i, 0)),
            full((1, D_MODEL)), full((D_MODEL, D_MODEL)), full((PLE_DIM, D_MODEL)),
        ],
        out_specs=pl.BlockSpec((tm, D_MODEL), lambda i: (i, 0)),
        out_shape=jax.ShapeDtypeStruct((n, D_MODEL), F32),
        compiler_params=_cparams(("parallel",)),
        name="ple",
    )(x2, peer2, p2, g, wg, wp)


def _relayout_w_in(w_in):
    o = np.cumsum([0, 512, 512, 512, Q_LORA, KV_LORA, MLA_ROPE, 512, 512, 512, 3 * D_MODEL])
    seg = lambda k: w_in[:, o[k]:o[k + 1]]
    pad = jnp.zeros((D_MODEL, 512 - Q_LORA - KV_LORA - MLA_ROPE), w_in.dtype)
    return jnp.concatenate([seg(9), seg(0), seg(1), seg(2), seg(6), seg(7), seg(8),
                            seg(3), seg(4), seg(5), pad], axis=1).astype(BF16)


def _pad_heads(w, width):
    k = w.shape[0]
    w3 = w.reshape(k, MLA_HEADS, width)
    return jnp.pad(w3, ((0, 0), (0, 0), (0, LANES - width))).reshape(k, MLA_HEADS * LANES)


def _pack_table(tab):
    bits = lax.bitcast_convert_type(tab.astype(BF16), jnp.uint16).astype(jnp.uint32)
    chunk = lambda c: bits[:, c * LANES:(c + 1) * LANES]
    words = [chunk(2 * s) | (chunk(2 * s + 1) << 16) for s in range(HALF_TILE)]
    return jnp.stack(words, axis=1).reshape(N_EXPERTS * HALF_TILE, LANES)


def _group_matrices():
    e = (np.arange(RCOLS)[None, :] // SUBLANES == np.arange(PEER_PAIRS)[:, None])
    return jnp.asarray(e, BF16), jnp.asarray(e.T, BF16)


def _rope_tables(seq):
    half = MLA_ROPE // 2
    freqs = ROPE_THETA ** (-np.arange(half, dtype=np.float32) / half)
    ang = np.arange(seq, dtype=np.float32)[:, None] * freqs[None, :]
    cos = np.ones((seq, LANES), np.float32)
    sin = np.zeros((seq, LANES), np.float32)
    cos[:, MLA_NOPE:MLA_NOPE + half] = np.cos(ang)
    cos[:, MLA_NOPE + half:MLA_QK] = np.cos(ang)
    sin[:, MLA_NOPE:MLA_NOPE + half] = -np.sin(ang)
    sin[:, MLA_NOPE + half:MLA_QK] = np.sin(ang)
    return jnp.asarray(cos), jnp.asarray(sin)


def _tile(n, pref):
    return pref if n % pref == 0 else n


def _layer(x2, p2, layer_idx, batch, seq, g_mix, w_in, g_diff_q, g_diff_k, lam_q1, lam_k1, lam_q2,
           lam_k2, g_diff_sub, g_cq, w_uq, g_ckv, w_ukv, g_mla_q, g_mla_k, conv_w, w_br_diff,
           w_br_mla, w_br_conv, w_out, g_ffn, w_query, sub_keys, u_experts, v_experts, g_ple,
           w_ple_gate, w_ple_proj):
    n = x2.shape[0]
    lambda_init = 0.8 - 0.6 * math.exp(-0.3 * layer_idx)
    tm = _tile(seq, 256)
    tq = _tile(seq, 512)
    row = lambda v: v.reshape(1, -1)

    z2 = _inproj(x2, row(g_mix), _relayout_w_in(w_in), tm)
    z3 = z2.reshape(batch, seq, Z_W)

    slopes = jnp.asarray(2.0 ** (-8.0 * np.arange(1, DIFF_HEADS + 1, dtype=np.float32) / DIFF_HEADS))
    gq2 = jnp.tile(g_diff_q, 2).reshape(1, LANES)
    gk2 = jnp.tile(g_diff_k, 2).reshape(1, LANES)
    lam4 = jnp.stack([lam_q1, lam_k1, lam_q2, lam_k2])
    yd = _diff_attention(z3, slopes, gq2, gk2, lam4, row(g_diff_sub), tq, lambda_init)

    w_ukv3 = w_ukv.reshape(KV_LORA, MLA_HEADS, MLA_NOPE + MLA_V)
    wuk_p = _pad_heads(w_ukv3[:, :, :MLA_NOPE].reshape(KV_LORA, -1), MLA_NOPE).astype(BF16)
    wuv_p = w_ukv3[:, :, MLA_NOPE:].reshape(KV_LORA, -1).astype(BF16)
    wuq_p = _pad_heads(w_uq, MLA_QK).astype(BF16)
    gq_p = jnp.pad(g_mla_q, (0, LANES - MLA_QK)).reshape(1, LANES)
    gk_p = jnp.pad(g_mla_k, (0, LANES - MLA_QK)).reshape(1, LANES)
    cos_t, sin_t = _rope_tables(seq)
    qm, km, vm = _mla_proj(z2, row(g_cq), wuq_p, row(g_ckv), wuk_p, wuv_p, gq_p, gk_p,
                           cos_t, sin_t, tm, seq)
    ym = _mla_attention(qm.reshape(batch, seq, -1), km.reshape(batch, seq, -1),
                        vm.reshape(batch, seq, -1), tq)

    x2 = _merge(x2, yd.reshape(n, DIFF_W), ym.reshape(n, -1), z2, conv_w,
                w_br_diff.astype(BF16), w_br_mla.astype(BF16), w_br_conv.astype(BF16),
                w_out.astype(BF16), tm, seq)

    h2, idx, gate = _route(x2, row(g_ffn), w_query.astype(BF16), sub_keys.astype(BF16),
                           _tile(seq, ROUTE_TM))
    tb = _tile(n, PEER_TB)
    e_mat, g_mat = _group_matrices()
    w = _peer_u(idx, _pack_table(u_experts), h2.reshape(n, SUBLANES, LANES), gate, g_mat, tb)
    peer = _peer_v(idx, _pack_table(v_experts), w, e_mat, tb)

    return _ple(x2, peer.reshape(n, D_MODEL), p2, row(g_ple), w_ple_gate.astype(BF16),
                w_ple_proj.astype(BF16), tm)


def kernel(x, p, g_mix, w_in, g_diff_q, g_diff_k, lam_q1, lam_k1, lam_q2, lam_k2, g_diff_sub, g_cq,
           w_uq, g_ckv, w_ukv, g_mla_q, g_mla_k, conv_w, w_br_diff, w_br_mla, w_br_conv, w_out,
           g_ffn, w_query, sub_keys, u_experts, v_experts, g_ple, w_ple_gate, w_ple_proj):
    batch, seq, _ = x.shape
    n = batch * seq
    x2 = x.reshape(n, D_MODEL)
    per_layer = (g_mix, w_in, g_diff_q, g_diff_k, lam_q1, lam_k1, lam_q2, lam_k2, g_diff_sub, g_cq,
                 w_uq, g_ckv, w_ukv, g_mla_q, g_mla_k, conv_w, w_br_diff, w_br_mla, w_br_conv,
                 w_out, g_ffn, w_query, sub_keys, u_experts, v_experts, g_ple, w_ple_gate,
                 w_ple_proj)
    for i in range(p.shape[0]):
        x2 = _layer(x2, p[i].reshape(n, PLE_DIM), i, batch, seq, *(a[i] for a in per_layer))
    return x2.reshape(batch, seq, D_MODEL)
```

```python
import functools
import math

import jax
import jax.numpy as jnp
import numpy as np
from jax import lax
from jax.experimental import pallas as pl
from jax.experimental.pallas import tpu as pltpu

F32 = jnp.float32
BF16 = jnp.bfloat16
I32 = jnp.int32

D_MODEL = 1024
DIFF_HEADS = 4
DIFF_HD = 64
DIFF_W = 512
MLA_HEADS = 8
MLA_NOPE = 64
MLA_ROPE = 32
MLA_V = 64
MLA_QK = 96
Q_LORA = 256
KV_LORA = 128
ROPE_THETA = 10000.0
CONV_W = 512
CONV_K = 3
PEER_HEADS = 8
N_KEYS = 128
N_EXPERTS = N_KEYS * N_KEYS
PEER_DK = 128
K_SUB = 16
PEER_TOPK = 16
PEER_PAIRS = PEER_HEADS * PEER_TOPK
PLE_DIM = 256
NORM_EPS = 1e-6
NEG_INF = -1e30

LANES = 128
SUBLANES = 8
HALF_TILE = SUBLANES // 2
VMEM_LIMIT = 56 * 1024 * 1024

Z_GATES = 0
Z_DQ = 3072
Z_DK = 3584
Z_DV = 4096
Z_CB = 4608
Z_CC = 5120
Z_CX = 5632
Z_MLA = 6144
Z_W = 6656


def _cparams(sem, vmem=VMEM_LIMIT):
    return pltpu.CompilerParams(dimension_semantics=sem, vmem_limit_bytes=vmem)


def _rms(x, g):
    return x * lax.rsqrt(jnp.mean(x * x, axis=-1, keepdims=True) + NORM_EPS) * g


def _inproj_kernel(x_ref, g_ref, w_ref, z_ref):
    h = _rms(x_ref[...], g_ref[...]).astype(BF16)
    z_ref[...] = jnp.dot(h, w_ref[...], preferred_element_type=F32)


def _inproj(x2, g, w_in_r, tm):
    n = x2.shape[0]
    return pl.pallas_call(
        _inproj_kernel,
        grid=(n // tm,),
        in_specs=[
            pl.BlockSpec((tm, D_MODEL), lambda i: (i, 0)),
            pl.BlockSpec((1, D_MODEL), lambda i: (0, 0)),
            pl.BlockSpec((D_MODEL, Z_W), lambda i: (0, 0)),
        ],
        out_specs=pl.BlockSpec((tm, Z_W), lambda i: (i, 0)),
        out_shape=jax.ShapeDtypeStruct((n, Z_W), F32),
        compiler_params=_cparams(("parallel",)),
        name="inproj",
    )(x2, g, w_in_r)


def _qk(q, kt):
    return lax.dot_general(q, kt, (((1,), (1,)), ((), ())), preferred_element_type=F32)


def _flash_update(s, m_ref, l_ref, acc_ref, vt, diag):
    tq, tk = s.shape
    if diag:
        row = lax.broadcasted_iota(I32, s.shape, 0)
        col = lax.broadcasted_iota(I32, s.shape, 1)
        s = jnp.where(row >= col, s, NEG_INF)
    m_old = m_ref[...]
    m_new = jnp.maximum(m_old, jnp.max(s, axis=-1, keepdims=True))
    p = jnp.exp(s - jnp.tile(m_new, (1, tk // LANES)))
    alpha = jnp.exp(m_old - m_new)
    l_ref[...] = alpha * l_ref[...] + jnp.sum(p, axis=-1, keepdims=True)
    m_ref[...] = m_new
    acc_ref[...] = alpha * acc_ref[...] + jnp.dot(p.astype(BF16), vt, preferred_element_type=F32)


def _flash_init(m_ref, l_ref, acc_ref):
    m_ref[...] = jnp.full(m_ref.shape, NEG_INF, F32)
    l_ref[...] = jnp.zeros(l_ref.shape, F32)
    acc_ref[...] = jnp.zeros(acc_ref.shape, F32)


def _flash_scratch(tq):
    return [pltpu.VMEM((tq, LANES), F32)] * 3


ALIBI_BITS = 6
ALIBI_SPLIT = 1 << ALIBI_BITS


def _diff_attn_kernel(slope_ref, q_ref, k_ref, v_ref, gq_ref, gk_ref, lam_ref, gsub_ref, o_ref,
                      k1_ref, k2_ref, m1_ref, l1_ref, a1_ref, m2_ref, l2_ref, a2_ref,
                      *, tq, seq, lambda_init):
    h = pl.program_id(1)
    qi = pl.program_id(2)
    lane = lax.broadcasted_iota(I32, (1, LANES), 1)
    first = lane < DIFF_HD
    slope = slope_ref[h]

    def halfnorm(x, g):
        x2 = x * x
        s_all = jnp.sum(x2, axis=-1, keepdims=True)
        s_lo = jnp.sum(jnp.where(first, x2, 0.0), axis=-1, keepdims=True)
        ms = jnp.where(first, s_lo, s_all - s_lo) * (1.0 / DIFF_HD)
        return x * lax.rsqrt(ms + NORM_EPS) * g

    def alibi_cols(pos0, base, query):
        pos = lax.broadcasted_iota(I32, (tq, 1), 0) + pos0
        hi = lax.shift_right_logical(pos, ALIBI_BITS).astype(F32) * (slope * ALIBI_SPLIT)
        lo = (pos & (ALIBI_SPLIT - 1)).astype(F32) * slope
        if query:
            c0, c1, c2, c3 = -hi, -lo, 1.0, 1.0
        else:
            c0, c1, c2, c3 = 1.0, 1.0, hi, lo
        return jnp.where(lane == base, c0, jnp.where(lane == base + 1, c1, jnp.where(
            lane == base + 2, c2, jnp.where(lane == base + 3, c3, 0.0))))

    @pl.when(qi == 0)
    def _():
        def body(c, carry):
            r0 = pl.multiple_of(c * tq, tq)
            kn = halfnorm(k_ref[pl.ds(r0, tq), :], gk_ref[...])
            k1_ref[pl.ds(r0, tq), :] = jnp.where(first, kn, alibi_cols(r0, DIFF_HD, False)).astype(BF16)
            k2_ref[pl.ds(r0, tq), :] = jnp.where(first, alibi_cols(r0, 0, False), kn).astype(BF16)
            return carry
        lax.fori_loop(0, seq // tq, body, 0)

    q = halfnorm(q_ref[...], gq_ref[...]) * (DIFF_HD ** -0.5)
    q1 = jnp.where(first, q, alibi_cols(qi * tq, DIFF_HD, True)).astype(BF16)
    q2 = jnp.where(first, alibi_cols(qi * tq, 0, True), q).astype(BF16)
    _flash_init(m1_ref, l1_ref, a1_ref)
    _flash_init(m2_ref, l2_ref, a2_ref)

    def step(j, diag):
        rows = pl.ds(pl.multiple_of(j * tq, tq), tq)
        vt = v_ref[rows, :].astype(BF16)
        _flash_update(_qk(q1, k1_ref[rows, :]), m1_ref, l1_ref, a1_ref, vt, diag)
        _flash_update(_qk(q2, k2_ref[rows, :]), m2_ref, l2_ref, a2_ref, vt, diag)

    def off_diag(j, carry):
        step(j, False)
        return carry

    lax.fori_loop(0, qi, off_diag, 0)
    step(qi, True)

    lam = lam_ref[...]
    e1 = jnp.exp(jnp.sum(lam[0:1] * lam[1:2], axis=-1, keepdims=True))
    e2 = jnp.exp(jnp.sum(lam[2:3] * lam[3:4], axis=-1, keepdims=True))
    lam_full = e1 - e2 + lambda_init
    o = a1_ref[...] / l1_ref[...] - lam_full * (a2_ref[...] / l2_ref[...])
    o_ref[...] = _rms(o, gsub_ref[...]) * (1.0 - lambda_init)


def _diff_attention(z3, slopes, gq2, gk2, lam4, gsub, tq, lambda_init):
    b, s, _ = z3.shape
    kern = functools.partial(_diff_attn_kernel, tq=tq, seq=s, lambda_init=lambda_init)
    return pl.pallas_call(
        kern,
        grid=(b, DIFF_HEADS, s // tq),
        in_specs=[
            pl.BlockSpec(memory_space=pltpu.SMEM),
            pl.BlockSpec((None, tq, LANES), lambda bi, h, qi: (bi, qi, Z_DQ // LANES + h)),
            pl.BlockSpec((None, s, LANES), lambda bi, h, qi: (bi, 0, Z_DK // LANES + h)),
            pl.BlockSpec((None, s, LANES), lambda bi, h, qi: (bi, 0, Z_DV // LANES + h)),
            pl.BlockSpec((1, LANES), lambda bi, h, qi: (0, 0)),
            pl.BlockSpec((1, LANES), lambda bi, h, qi: (0, 0)),
            pl.BlockSpec((4, DIFF_HD), lambda bi, h, qi: (0, 0)),
            pl.BlockSpec((1, LANES), lambda bi, h, qi: (0, 0)),
        ],
        out_specs=pl.BlockSpec((None, tq, LANES), lambda bi, h, qi: (bi, qi, h)),
        out_shape=jax.ShapeDtypeStruct((b, s, DIFF_W), F32),
        scratch_shapes=[pltpu.VMEM((s, LANES), BF16), pltpu.VMEM((s, LANES), BF16)]
        + _flash_scratch(tq) + _flash_scratch(tq),
        compiler_params=_cparams(("parallel", "parallel", "arbitrary")),
        name="diff_attn",
    )(slopes, z3, z3, z3, gq2, gk2, lam4, gsub)


def _mla_proj_kernel(z_ref, gcq_ref, wuq_ref, gckv_ref, wuk_ref, wuv_ref, gq_ref, gk_ref,
                     cos_ref, sin_ref, qm_ref, km_ref, vm_ref):
    z = z_ref[...]
    cq = z[:, :Q_LORA]
    ckv = z[:, Q_LORA:Q_LORA + KV_LORA]
    krb = z[:, Q_LORA + KV_LORA:]
    cqn = _rms(cq, gcq_ref[...]).astype(BF16)
    ckvn = _rms(ckv, gckv_ref[...]).astype(BF16)
    q = jnp.dot(cqn, wuq_ref[...], preferred_element_type=F32)
    kn = jnp.dot(ckvn, wuk_ref[...], preferred_element_type=F32)
    v = jnp.dot(ckvn, wuv_ref[...], preferred_element_type=F32)
    vm_ref[...] = v.astype(BF16)

    lane = lax.broadcasted_iota(I32, (1, LANES), 1)
    in_rope = (lane >= MLA_NOPE) & (lane < MLA_QK)
    kr = jnp.where(in_rope, pltpu.roll(krb, MLA_NOPE, 1), 0.0)
    cos = cos_ref[...]
    sin = sin_ref[...]
    gq = gq_ref[...]
    gk = gk_ref[...]
    half = MLA_ROPE // 2
    src = lax.broadcasted_iota(I32, (LANES, LANES), 0)
    dst = lax.broadcasted_iota(I32, (LANES, LANES), 1)
    x1_dst = (dst >= MLA_NOPE) & (dst < MLA_NOPE + half)
    x2_dst = (dst >= MLA_NOPE + half) & (dst < MLA_QK)
    swap_m = jnp.where((x1_dst & (src == dst + half)) | (x2_dst & (src == dst - half)),
                       1.0, 0.0).astype(BF16)
    ones_m = jnp.ones((LANES, LANES), BF16)

    def mm_exactish(x, m):
        hi = x.astype(BF16)
        lo = (x - hi.astype(F32)).astype(BF16)
        return (jnp.dot(hi, m, preferred_element_type=F32)
                + jnp.dot(lo, m, preferred_element_type=F32))

    def norm_rope(xh, g):
        ms = mm_exactish(xh * xh, ones_m) * (1.0 / MLA_QK)
        xn = xh * lax.rsqrt(ms + NORM_EPS) * g
        return xn * cos + mm_exactish(xn, swap_m) * sin

    for hh in range(MLA_HEADS):
        sl = slice(hh * LANES, (hh + 1) * LANES)
        qm_ref[:, sl] = (norm_rope(q[:, sl], gq) * (MLA_QK ** -0.5)).astype(BF16)
        km_ref[:, sl] = norm_rope(kn[:, sl] + kr, gk).astype(BF16)


def _mla_proj(z2, gcq, wuq_p, gckv, wuk_p, wuv_p, gq_p, gk_p, cos_t, sin_t, tm, seq):
    n = z2.shape[0]
    nsb = seq // tm
    full = lambda shp: pl.BlockSpec(shp, lambda i: (0, 0))
    return pl.pallas_call(
        _mla_proj_kernel,
        grid=(n // tm,),
        in_specs=[
            pl.BlockSpec((tm, 512), lambda i: (i, Z_MLA // 512)),
            full((1, Q_LORA)), full((Q_LORA, MLA_HEADS * LANES)),
            full((1, KV_LORA)), full((KV_LORA, MLA_HEADS * LANES)), full((KV_LORA, MLA_HEADS * MLA_V)),
            full((1, LANES)), full((1, LANES)),
            pl.BlockSpec((tm, LANES), lambda i: (i % nsb, 0)),
            pl.BlockSpec((tm, LANES), lambda i: (i % nsb, 0)),
        ],
        out_specs=[
            pl.BlockSpec((tm, MLA_HEADS * LANES), lambda i: (i, 0)),
            pl.BlockSpec((tm, MLA_HEADS * LANES), lambda i: (i, 0)),
            pl.BlockSpec((tm, MLA_HEADS * MLA_V), lambda i: (i, 0)),
        ],
        out_shape=[
            jax.ShapeDtypeStruct((n, MLA_HEADS * LANES), BF16),
            jax.ShapeDtypeStruct((n, MLA_HEADS * LANES), BF16),
            jax.ShapeDtypeStruct((n, MLA_HEADS * MLA_V), BF16),
        ],
        compiler_params=_cparams(("parallel",)),
        name="mla_proj",
    )(z2, gcq, wuq_p, gckv, wuk_p, wuv_p, gq_p, gk_p, cos_t, sin_t)


def _mla_attn_kernel(q_ref, k_ref, v_ref, o_ref, ma_ref, la_ref, aa_ref, mb_ref, lb_ref, ab_ref,
                     *, tq):
    qi = pl.program_id(2)
    _flash_init(ma_ref, la_ref, aa_ref)
    _flash_init(mb_ref, lb_ref, ab_ref)

    def step(j, diag):
        rows = pl.ds(pl.multiple_of(j * tq, tq), tq)
        vt = v_ref[rows, :]
        _flash_update(_qk(q_ref[:, :LANES], k_ref[rows, :LANES]), ma_ref, la_ref, aa_ref, vt, diag)
        _flash_update(_qk(q_ref[:, LANES:], k_ref[rows, LANES:]), mb_ref, lb_ref, ab_ref, vt, diag)

    def off_diag(j, carry):
        step(j, False)
        return carry

    lax.fori_loop(0, qi, off_diag, 0)
    step(qi, True)
    lane = lax.broadcasted_iota(I32, (1, LANES), 1)
    o_ref[...] = jnp.where(lane < MLA_V, aa_ref[...] / la_ref[...], ab_ref[...] / lb_ref[...])


def _mla_attention(qm3, km3, vm3, tq):
    b, s, _ = qm3.shape
    return pl.pallas_call(
        functools.partial(_mla_attn_kernel, tq=tq),
        grid=(b, MLA_HEADS // 2, s // tq),
        in_specs=[
            pl.BlockSpec((None, tq, 2 * LANES), lambda bi, hp, qi: (bi, qi, hp)),
            pl.BlockSpec((None, s, 2 * LANES), lambda bi, hp, qi: (bi, 0, hp)),
            pl.BlockSpec((None, s, LANES), lambda bi, hp, qi: (bi, 0, hp)),
        ],
        out_specs=pl.BlockSpec((None, tq, LANES), lambda bi, hp, qi: (bi, qi, hp)),
        out_shape=jax.ShapeDtypeStruct((b, s, MLA_HEADS * MLA_V), F32),
        scratch_shapes=_flash_scratch(tq) + _flash_scratch(tq),
        compiler_params=_cparams(("parallel", "parallel", "arbitrary")),
        name="mla_attn",
    )(qm3, km3, vm3)


def _merge_kernel(x_ref, yd_ref, ym_ref, cb_ref, cc_ref, cx_ref, cch_ref, cxh_ref,
                  g0_ref, g1_ref, g2_ref, cw_ref, wd_ref, wm_ref, wc_ref, wo_ref,
                  o_ref, ucat_ref, *, tm, seq):
    i = pl.program_id(0)
    at_seq_start = (i * tm) % seq == 0
    halo = cch_ref[...] * cxh_ref[...]
    ucat_ref[0:SUBLANES, :] = jnp.where(at_seq_start, 0.0, halo)
    ucat_ref[SUBLANES:, :] = cc_ref[...] * cx_ref[...]
    cw = cw_ref[...]
    conv = (cw[0:1] * ucat_ref[SUBLANES - 2:SUBLANES - 2 + tm, :]
            + cw[1:2] * ucat_ref[SUBLANES - 1:SUBLANES - 1 + tm, :]
            + cw[2:3] * ucat_ref[SUBLANES:, :])
    yc = cb_ref[...] * conv

    def br(y, w_ref, g_ref):
        return jax.nn.sigmoid(g_ref[...]) * jnp.dot(y.astype(BF16), w_ref[...],
                                                    preferred_element_type=F32)

    mixed = br(yd_ref[...], wd_ref, g0_ref) + br(ym_ref[...], wm_ref, g1_ref) + br(yc, wc_ref, g2_ref)
    o_ref[...] = x_ref[...] + jnp.dot(mixed.astype(BF16), wo_ref[...], preferred_element_type=F32)


def _merge(x2, yd2, ym2, z2, conv_w, wd, wm, wc, wo, tm, seq):
    n = x2.shape[0]
    hb = tm // SUBLANES
    full = lambda shp: pl.BlockSpec(shp, lambda i: (0, 0))
    zc = lambda off: pl.BlockSpec((tm, 512), lambda i: (i, off // 512))
    zh = lambda off: pl.BlockSpec((SUBLANES, 512), lambda i: (jnp.maximum(i * hb - 1, 0), off // 512))
    zg = lambda k: pl.BlockSpec((tm, D_MODEL), lambda i: (i, k))
    return pl.pallas_call(
        functools.partial(_merge_kernel, tm=tm, seq=seq),
        grid=(n // tm,),
        in_specs=[
            pl.BlockSpec((tm, D_MODEL), lambda i: (i, 0)),
            pl.BlockSpec((tm, DIFF_W), lambda i: (i, 0)),
            pl.BlockSpec((tm, 512), lambda i: (i, 0)),
            zc(Z_CB), zc(Z_CC), zc(Z_CX), zh(Z_CC), zh(Z_CX),
            zg(0), zg(1), zg(2),
            full((CONV_K, CONV_W)),
            full((DIFF_W, D_MODEL)), full((512, D_MODEL)), full((CONV_W, D_MODEL)),
            full((D_MODEL, D_MODEL)),
        ],
        out_specs=pl.BlockSpec((tm, D_MODEL), lambda i: (i, 0)),
        out_shape=jax.ShapeDtypeStruct((n, D_MODEL), F32),
        scratch_shapes=[pltpu.VMEM((tm + SUBLANES, CONV_W), F32)],
        compiler_params=_cparams(("parallel",)),
        name="merge",
    )(x2, yd2, ym2, z2, z2, z2, z2, z2, z2, z2, z2, conv_w, wd, wm, wc, wo)


ROUTE_TM = 512


def _topk_rows(s, k):
    rows, t = s.shape
    riota = lax.broadcasted_iota(I32, (rows, t), 0)
    kiota = lax.broadcasted_iota(I32, (k, t), 0)
    vals = jnp.zeros((k, t), F32)
    idxs = jnp.zeros((k, t), I32)
    for i in range(k):
        m = jnp.max(s, axis=0, keepdims=True)
        am = jnp.min(jnp.where(s == m, riota, rows), axis=0, keepdims=True)
        vals = jnp.where(kiota == i, m, vals)
        idxs = jnp.where(kiota == i, am, idxs)
        s = jnp.where(riota == am, -jnp.inf, s)
    return vals, idxs


def _route_kernel(x_ref, g_ref, wq_ref, keys_ref, h_ref, idx_ref, gate_ref,
                  q_ref, expt_ref, gatet_ref, *, tm):
    h2 = _rms(x_ref[...], g_ref[...])
    h_ref[...] = h2
    q_ref[...] = jnp.dot(h2.astype(BF16), wq_ref[...], preferred_element_type=F32)

    def head(hh, carry):
        tops = []
        for p in range(2):
            c0 = pl.multiple_of(hh * (2 * PEER_DK) + p * PEER_DK, PEER_DK)
            qhp = q_ref[:, pl.ds(c0, PEER_DK)].astype(BF16)
            keys = keys_ref[p, hh]
            st = _qk(keys, qhp)
            tops.append(_topk_rows(st, K_SUB))
        (s0, i0), (s1, i1) = tops
        nb = lambda a: K_SUB if a == 0 else SUBLANES
        cand = jnp.concatenate([s0[a:a + 1] + s1[:nb(a)] for a in range(K_SUB)], axis=0)
        nc = K_SUB + (K_SUB - 1) * SUBLANES
        riota = lax.broadcasted_iota(I32, (nc, tm), 0)
        kiota = lax.broadcasted_iota(I32, (PEER_TOPK, tm), 0)
        best = jnp.zeros((PEER_TOPK, tm), F32)
        brow = jnp.zeros((PEER_TOPK, tm), I32)
        for i in range(PEER_TOPK):
            m = jnp.max(cand, axis=0, keepdims=True)
            am = jnp.min(jnp.where(cand == m, riota, nc), axis=0, keepdims=True)
            best = jnp.where(kiota == i, m, best)
            brow = jnp.where(kiota == i, am, brow)
            cand = jnp.where(riota == am, -jnp.inf, cand)
        tail = brow - K_SUB
        a_sel = jnp.where(tail < 0, 0, (tail >> 3) + 1)
        b_sel = jnp.where(tail < 0, brow, tail & (SUBLANES - 1))
        k0 = jnp.zeros((PEER_TOPK, tm), I32)
        k1 = jnp.zeros((PEER_TOPK, tm), I32)
        for a in range(K_SUB):
            k0 = jnp.where(a_sel == a, i0[a:a + 1], k0)
            k1 = jnp.where(b_sel == a, i1[a:a + 1], k1)
        bexp = k0 * N_KEYS + k1
        ex = jnp.exp(best - best[0:1])
        gate = ex / jnp.sum(ex, axis=0, keepdims=True)
        r0 = pl.multiple_of(hh * PEER_TOPK, PEER_TOPK)
        expt_ref[pl.ds(r0, PEER_TOPK), :] = bexp
        gatet_ref[pl.ds(r0, PEER_TOPK), :] = gate
        return carry

    lax.fori_loop(0, PEER_HEADS, head, 0)
    idx_ref[...] = expt_ref[...].T * HALF_TILE
    gate_ref[...] = gatet_ref[...].T


def _route(x2, g, wq, keys, tm):
    n = x2.shape[0]
    return pl.pallas_call(
        functools.partial(_route_kernel, tm=tm),
        grid=(n // tm,),
        in_specs=[
            pl.BlockSpec((tm, D_MODEL), lambda i: (i, 0)),
            pl.BlockSpec((1, D_MODEL), lambda i: (0, 0)),
            pl.BlockSpec((D_MODEL, 2 * PEER_DK * PEER_HEADS), lambda i: (0, 0)),
            pl.BlockSpec((2, PEER_HEADS, N_KEYS, PEER_DK), lambda i: (0, 0, 0, 0)),
        ],
        out_specs=[
            pl.BlockSpec((tm, D_MODEL), lambda i: (i, 0)),
            pl.BlockSpec((tm, PEER_PAIRS), lambda i: (i, 0)),
            pl.BlockSpec((tm, PEER_PAIRS), lambda i: (i, 0)),
        ],
        out_shape=[
            jax.ShapeDtypeStruct((n, D_MODEL), F32),
            jax.ShapeDtypeStruct((n, PEER_PAIRS), I32),
            jax.ShapeDtypeStruct((n, PEER_PAIRS), F32),
        ],
        scratch_shapes=[
            pltpu.VMEM((tm, 2 * PEER_DK * PEER_HEADS), F32),
            pltpu.VMEM((PEER_PAIRS, tm), I32),
            pltpu.VMEM((PEER_PAIRS, tm), F32),
        ],
        compiler_params=_cparams(("parallel",)),
        name="peer_route",
    )(x2, g, wq, keys)


RCOLS = PEER_PAIRS * SUBLANES
PEER_TB = 64


def _table_spec():
    return pl.BlockSpec((N_EXPERTS * HALF_TILE, LANES), lambda i: (0, 0),
                        pipeline_mode=pl.Buffered(1))


def _gather_rows(tab_ref, idx_ref, t):
    idx_t = idx_ref.at[t]
    tiles = [tab_ref[pl.ds(pl.multiple_of(idx_t[k], HALF_TILE), HALF_TILE), :]
             for k in range(PEER_PAIRS)]
    return pltpu.bitcast(jnp.concatenate(tiles, axis=0), BF16)


def _selector():
    s = lax.broadcasted_iota(I32, (SUBLANES, RCOLS), 0)
    c = lax.broadcasted_iota(I32, (SUBLANES, RCOLS), 1)
    return (c & (SUBLANES - 1)) == s


def _peer_u_kernel(idx_ref, tab_ref, h_ref, gate_ref, g_ref, w_ref, res_ref, *, tb):
    keep = _selector()

    for t in range(tb):
        r = _gather_rows(tab_ref, idx_ref, t)
        res = _qk(h_ref[t].astype(BF16), r)
        res_ref[t] = jnp.where(keep, res, 0.0)
    res = res_ref[...].reshape(tb * SUBLANES, RCOLS)
    res_hi = res.astype(BF16)
    res_lo = (res - res_hi.astype(F32)).astype(BF16)
    part = (jnp.dot(res_hi, g_ref[...], preferred_element_type=F32)
            + jnp.dot(res_lo, g_ref[...], preferred_element_type=F32))
    act = jnp.sum(part.reshape(tb, SUBLANES, PEER_PAIRS), axis=1)
    gelu = 0.5 * act * (1.0 + lax.erf(act * (2.0 ** -0.5)))
    w_ref[...] = gate_ref[...] * gelu


def _peer_u(idx, tab, h3, gate, g_mat, tb):
    n = idx.shape[0]
    return pl.pallas_call(
        functools.partial(_peer_u_kernel, tb=tb),
        grid=(n // tb,),
        in_specs=[
            pl.BlockSpec((tb, PEER_PAIRS), lambda i: (i, 0), memory_space=pltpu.SMEM),
            _table_spec(),
            pl.BlockSpec((tb, SUBLANES, LANES), lambda i: (i, 0, 0)),
            pl.BlockSpec((tb, PEER_PAIRS), lambda i: (i, 0)),
            pl.BlockSpec((RCOLS, PEER_PAIRS), lambda i: (0, 0)),
        ],
        out_specs=pl.BlockSpec((tb, PEER_PAIRS), lambda i: (i, 0)),
        out_shape=jax.ShapeDtypeStruct((n, PEER_PAIRS), F32),
        scratch_shapes=[pltpu.VMEM((tb, SUBLANES, RCOLS), F32)],
        compiler_params=_cparams(("arbitrary",)),
        name="peer_u",
    )(idx, tab, h3, gate, g_mat)


def _peer_v_kernel(idx_ref, tab_ref, w_ref, e_ref, o_ref, wrep_ref, *, tb):
    wrep_ref[...] = jnp.dot(w_ref[...].astype(BF16), e_ref[...], preferred_element_type=F32)
    keep = _selector()

    for t in range(tb):
        r = _gather_rows(tab_ref, idx_ref, t)
        sel = jnp.where(keep, wrep_ref[pl.ds(t, 1), :], 0.0)
        o_ref[t] = jnp.dot(sel.astype(BF16), r, preferred_element_type=F32)


def _peer_v(idx, tab, w, e_mat, tb):
    n = idx.shape[0]
    return pl.pallas_call(
        functools.partial(_peer_v_kernel, tb=tb),
        grid=(n // tb,),
        in_specs=[
            pl.BlockSpec((tb, PEER_PAIRS), lambda i: (i, 0), memory_space=pltpu.SMEM),
            _table_spec(),
            pl.BlockSpec((tb, PEER_PAIRS), lambda i: (i, 0)),
            pl.BlockSpec((PEER_PAIRS, RCOLS), lambda i: (0, 0)),
        ],
        out_specs=pl.BlockSpec((tb, SUBLANES, LANES), lambda i: (i, 0, 0)),
        out_shape=jax.ShapeDtypeStruct((n, SUBLANES, LANES), F32),
        scratch_shapes=[pltpu.VMEM((tb, RCOLS), F32)],
        compiler_params=_cparams(("arbitrary",)),
        name="peer_v",
    )(idx, tab, w, e_mat)


def _ple_kernel(x_ref, peer_ref, p_ref, g_ref, wg_ref, wp_ref, o_ref):
    x = x_ref[...] + peer_ref[...]
    gate = jax.nn.sigmoid(jnp.dot(_rms(x, g_ref[...]).astype(BF16), wg_ref[...],
                                  preferred_element_type=F32))
    emb = jnp.dot(p_ref[...].astype(BF16), wp_ref[...], preferred_element_type=F32)
    o_ref[...] = x + gate * emb


def _ple(x2, peer2, p2, g, wg, wp, tm):
    n = x2.shape[0]
    full = lambda shp: pl.BlockSpec(shp, lambda i: (0, 0))
    return pl.pallas_call(
        _ple_kernel,
        grid=(n // tm,),
        in_specs=[
            pl.BlockSpec((tm, D_MODEL), lambda i: (i, 0)),
            pl.BlockSpec((tm, D_MODEL), lambda i: (i, 0)),
            pl.BlockSpec((tm, PLE_DIM), lambda i: (i, 0)),
            full((1, D_MODEL)), full((D_MODEL, D_MODEL)), full((PLE_DIM, D_MODEL)),
        ],
        out_specs=pl.BlockSpec((tm, D_MODEL), lambda i: (i, 0)),
        out_shape=jax.ShapeDtypeStruct((n, D_MODEL), F32),
        compiler_params=_cparams(("parallel",)),
        name="ple",
    )(x2, peer2, p2, g, wg, wp)


def _relayout_w_in(w_in):
    o = np.cumsum([0, 512, 512, 512, Q_LORA, KV_LORA, MLA_ROPE, 512, 512, 512, 3 * D_MODEL])
    seg = lambda k: w_in[:, o[k]:o[k + 1]]
    pad = jnp.zeros((D_MODEL, 512 - Q_LORA - KV_LORA - MLA_ROPE), w_in.dtype)
    return jnp.concatenate([seg(9), seg(0), seg(1), seg(2), seg(6), seg(7), seg(8),
                            seg(3), seg(4), seg(5), pad], axis=1).astype(BF16)


def _pad_heads(w, width):
    k = w.shape[0]
    w3 = w.reshape(k, MLA_HEADS, width)
    return jnp.pad(w3, ((0, 0), (0, 0), (0, LANES - width))).reshape(k, MLA_HEADS * LANES)


def _pack_table(tab):
    bits = lax.bitcast_convert_type(tab.astype(BF16), jnp.uint16).astype(jnp.uint32)
    chunk = lambda c: bits[:, c * LANES:(c + 1) * LANES]
    words = [chunk(2 * s) | (chunk(2 * s + 1) << 16) for s in range(HALF_TILE)]
    return jnp.stack(words, axis=1).reshape(N_EXPERTS * HALF_TILE, LANES)


def _group_matrices():
    e = (np.arange(RCOLS)[None, :] // SUBLANES == np.arange(PEER_PAIRS)[:, None])
    return jnp.asarray(e, BF16), jnp.asarray(e.T, BF16)


def _rope_tables(seq):
    half = MLA_ROPE // 2
    freqs = ROPE_THETA ** (-np.arange(half, dtype=np.float32) / half)
    ang = np.arange(seq, dtype=np.float32)[:, None] * freqs[None, :]
    cos = np.ones((seq, LANES), np.float32)
    sin = np.zeros((seq, LANES), np.float32)
    cos[:, MLA_NOPE:MLA_NOPE + half] = np.cos(ang)
    cos[:, MLA_NOPE + half:MLA_QK] = np.cos(ang)
    sin[:, MLA_NOPE:MLA_NOPE + half] = -np.sin(ang)
    sin[:, MLA_NOPE + half:MLA_QK] = np.sin(ang)
    return jnp.asarray(cos), jnp.asarray(sin)


def _tile(n, pref):
    return pref if n % pref == 0 else n


def _layer(x2, p2, layer_idx, batch, seq, g_mix, w_in, g_diff_q, g_diff_k, lam_q1, lam_k1, lam_q2,
           lam_k2, g_diff_sub, g_cq, w_uq, g_ckv, w_ukv, g_mla_q, g_mla_k, conv_w, w_br_diff,
           w_br_mla, w_br_conv, w_out, g_ffn, w_query, sub_keys, u_experts, v_experts, g_ple,
           w_ple_gate, w_ple_proj):
    n = x2.shape[0]
    lambda_init = 0.8 - 0.6 * math.exp(-0.3 * layer_idx)
    tm = _tile(seq, 256)
    tq = _tile(seq, 512)
    row = lambda v: v.reshape(1, -1)

    z2 = _inproj(x2, row(g_mix), _relayout_w_in(w_in), tm)
    z3 = z2.reshape(batch, seq, Z_W)

    slopes = jnp.asarray(2.0 ** (-8.0 * np.arange(1, DIFF_HEADS + 1, dtype=np.float32) / DIFF_HEADS))
    gq2 = jnp.tile(g_diff_q, 2).reshape(1, LANES)
    gk2 = jnp.tile(g_diff_k, 2).reshape(1, LANES)
    lam4 = jnp.stack([lam_q1, lam_k1, lam_q2, lam_k2])
    yd = _diff_attention(z3, slopes, gq2, gk2, lam4, row(g_diff_sub), tq, lambda_init)

    w_ukv3 = w_ukv.reshape(KV_LORA, MLA_HEADS, MLA_NOPE + MLA_V)
    wuk_p = _pad_heads(w_ukv3[:, :, :MLA_NOPE].reshape(KV_LORA, -1), MLA_NOPE).astype(BF16)
    wuv_p = w_ukv3[:, :, MLA_NOPE:].reshape(KV_LORA, -1).astype(BF16)
    wuq_p = _pad_heads(w_uq, MLA_QK).astype(BF16)
    gq_p = jnp.pad(g_mla_q, (0, LANES - MLA_QK)).reshape(1, LANES)
    gk_p = jnp.pad(g_mla_k, (0, LANES - MLA_QK)).reshape(1, LANES)
    cos_t, sin_t = _rope_tables(seq)
    qm, km, vm = _mla_proj(z2, row(g_cq), wuq_p, row(g_ckv), wuk_p, wuv_p, gq_p, gk_p,
                           cos_t, sin_t, tm, seq)
    ym = _mla_attention(qm.reshape(batch, seq, -1), km.reshape(batch, seq, -1),
                        vm.reshape(batch, seq, -1), tq)

    x2 = _merge(x2, yd.reshape(n, DIFF_W), ym.reshape(n, -1), z2, conv_w,
                w_br_diff.astype(BF16), w_br_mla.astype(BF16), w_br_conv.astype(BF16),
                w_out.astype(BF16), tm, seq)

    h2, idx, gate = _route(x2, row(g_ffn), w_query.astype(BF16), sub_keys.astype(BF16),
                           _tile(seq, ROUTE_TM))
    tb = _tile(n, PEER_TB)
    e_mat, g_mat = _group_matrices()
    w = _peer_u(idx, _pack_table(u_experts), h2.reshape(n, SUBLANES, LANES), gate, g_mat, tb)
    peer = _peer_v(idx, _pack_table(v_experts), w, e_mat, tb)

    return _ple(x2, peer.reshape(n, D_MODEL), p2, row(g_ple), w_ple_gate.astype(BF16),
                w_ple_proj.astype(BF16), tm)


def kernel(x, p, g_mix, w_in, g_diff_q, g_diff_k, lam_q1, lam_k1, lam_q2, lam_k2, g_diff_sub, g_cq,
           w_uq, g_ckv, w_ukv, g_mla_q, g_mla_k, conv_w, w_br_diff, w_br_mla, w_br_conv, w_out,
           g_ffn, w_query, sub_keys, u_experts, v_experts, g_ple, w_ple_gate, w_ple_proj):
    batch, seq, _ = x.shape
    n = batch * seq
    x2 = x.reshape(n, D_MODEL)
    per_layer = (g_mix, w_in, g_diff_q, g_diff_k, lam_q1, lam_k1, lam_q2, lam_k2, g_diff_sub, g_cq,
                 w_uq, g_ckv, w_ukv, g_mla_q, g_mla_k, conv_w, w_br_diff, w_br_mla, w_br_conv,
                 w_out, g_ffn, w_query, sub_keys, u_experts, v_experts, g_ple, w_ple_gate,
                 w_ple_proj)
    for i in range(p.shape[0]):
        x2 = _layer(x2, p[i].reshape(n, PLE_DIM), i, batch, seq, *(a[i] for a in per_layer))
    return x2.reshape(batch, seq, D_MODEL)
```
